```python
import math
import jax, jax.numpy as jnp
from jax import lax
import numpy as np

D_MODEL = 1024
BATCH = 1
SEQ = 16384
DEPTH = 2

D_MIX = D_MODEL
GM_HEADS = 4
GM_WIDTH = D_MIX // 4
GM_HEAD_DIM = GM_WIDTH // GM_HEADS
CHUNK = 128
DA_HEADS = 4
DA_WIDTH = D_MIX // 2
DA_V_DIM = DA_WIDTH // DA_HEADS
DA_QK_DIM = DA_V_DIM // 2
Q_BLOCK = 128
CV_GROUPS = 4
CV_WIDTH = D_MIX // 4
CV_KERNEL = 31
D_FF = 2816
FFN_KERNEL = 3
N_BUCKETS = 32
MAX_DISTANCE = 128
EPS = 1e-6

GM_IN = 2 * GM_WIDTH
DA_Q_IN = DA_HEADS * 2 * DA_QK_DIM
DA_K_IN = DA_HEADS * 2 * DA_QK_DIM
DA_V_IN = DA_HEADS * DA_V_DIM
CV_IN = 2 * CV_WIDTH
IN_WIDTH = GM_IN + DA_Q_IN + DA_K_IN + DA_V_IN + CV_IN
SPLITS = tuple(np.cumsum([GM_IN, DA_Q_IN, DA_K_IN, DA_V_IN])[:].tolist())

kernel_name = "hybrid_gmlp_diffattn_conformer_block"


def rms_norm(x, g):
    xf = x.astype(jnp.float32)
    y = xf * lax.rsqrt(jnp.mean(xf * xf, axis=-1, keepdims=True) + EPS)
    return (y * g).astype(x.dtype)


def layer_norm(x, g, b):
    xf = x.astype(jnp.float32)
    mu = jnp.mean(xf, axis=-1, keepdims=True)
    var = jnp.mean(jnp.square(xf - mu), axis=-1, keepdims=True)
    return ((xf - mu) * lax.rsqrt(var + EPS) * g + b).astype(x.dtype)


def group_layer_norm(x, g, b, groups):
    B, S, C = x.shape
    xf = x.astype(jnp.float32).reshape(B, S, groups, C // groups)
    mu = jnp.mean(xf, axis=-1, keepdims=True)
    var = jnp.mean(jnp.square(xf - mu), axis=-1, keepdims=True)
    y = ((xf - mu) * lax.rsqrt(var + EPS)).reshape(B, S, C)
    return (y * g + b).astype(x.dtype)


def causal_depthwise_conv(x, w, b):
    K, C = w.shape
    y = lax.conv_general_dilated(
        x, w[:, None, :].astype(x.dtype), window_strides=(1,),
        padding=[(K - 1, 0)], dimension_numbers=("NWC", "WIO", "NWC"),
        feature_group_count=C)
    return y + b


def t5_causal_bucket(rel):
    n = jnp.maximum(rel, 0)
    max_exact = N_BUCKETS // 2
    nf = jnp.maximum(n, 1).astype(jnp.float32)
    large = max_exact + (jnp.log(nf / max_exact) / math.log(MAX_DISTANCE / max_exact)
                         * (N_BUCKETS - max_exact)).astype(jnp.int32)
    large = jnp.minimum(large, N_BUCKETS - 1)
    return jnp.where(n < max_exact, n, large)


def chunked_spatial_gating(u, v, w_s, b_s, ln_g, ln_b):
    B, S, _ = v.shape
    v = layer_norm(v, ln_g, ln_b)
    vc = v.reshape(B, S // CHUNK, CHUNK, GM_HEADS, GM_HEAD_DIM)
    causal = jnp.tril(jnp.ones((CHUNK, CHUNK), dtype=bool))
    w = jnp.where(causal[None], w_s, jnp.zeros_like(w_s))
    mixed = jnp.einsum("hts,bcshd->bcthd", w, vc) + jnp.transpose(b_s)[None, None, :, :, None]
    return u * mixed.reshape(B, S, GM_WIDTH)


def differential_attention(q1, q2, k1, k2, v, lam, rel_bias):
    B, S, H, _ = q1.shape
    n_blocks = S // Q_BLOCK
    scale = DA_QK_DIM ** -0.5
    k_pos = jnp.arange(S, dtype=jnp.int32)

    def block(i):
        start = i * Q_BLOCK
        qb1 = lax.dynamic_slice_in_dim(q1, start, Q_BLOCK, axis=1)
        qb2 = lax.dynamic_slice_in_dim(q2, start, Q_BLOCK, axis=1)
        q_pos = start + jnp.arange(Q_BLOCK, dtype=jnp.int32)
        rel = q_pos[:, None] - k_pos[None, :]
        bias = jnp.transpose(rel_bias[t5_causal_bucket(rel)], (2, 0, 1)).astype(jnp.float32)
        visible = rel >= 0
        s1 = jnp.einsum("bqhd,bkhd->bhqk", qb1, k1).astype(jnp.float32) * scale + bias
        s2 = jnp.einsum("bqhd,bkhd->bhqk", qb2, k2).astype(jnp.float32) * scale + bias
        p1 = jax.nn.softmax(jnp.where(visible, s1, -jnp.inf), axis=-1)
        p2 = jax.nn.softmax(jnp.where(visible, s2, -jnp.inf), axis=-1)
        a = (p1 - lam * p2).astype(v.dtype)
        return jnp.einsum("bhqk,bkhd->bqhd", a, v)

    out = lax.map(block, jnp.arange(n_blocks, dtype=jnp.int32))
    return jnp.transpose(out, (1, 0, 2, 3, 4)).reshape(B, S, H, DA_V_DIM)


def conformer_conv_module(a, g, dw_w, dw_b, ln_g, ln_b):
    h = a * jax.nn.sigmoid(g)
    h = causal_depthwise_conv(h, dw_w, dw_b)
    h = group_layer_norm(h, ln_g, ln_b, CV_GROUPS)
    return jax.nn.silu(h)


def setup_inputs(seed: int = 0) -> dict:
    key = jax.random.key(seed)
    ks = jax.random.split(key, 32)
    f32 = jnp.float32
    nrm = lambda k, shape, s: jax.random.normal(k, shape, f32) * s
    gain = lambda k, shape: 1.0 + 0.02 * jax.random.normal(k, shape, f32)
    return {
        "x": jax.random.normal(ks[0], (BATCH, SEQ, D_MODEL), f32),
        "w_in": nrm(ks[1], (DEPTH, D_MODEL, IN_WIDTH), D_MODEL ** -0.5),
        "w_out": nrm(ks[2], (DEPTH, D_MIX, D_MODEL), D_MIX ** -0.5),
        "gm_ln_g": gain(ks[3], (DEPTH, GM_WIDTH)),
        "gm_ln_b": nrm(ks[4], (DEPTH, GM_WIDTH), 0.02),
        "gm_w_s": nrm(ks[5], (DEPTH, GM_HEADS, CHUNK, CHUNK), CHUNK ** -0.5),
        "gm_b_s": gain(ks[6], (DEPTH, GM_HEADS, CHUNK)),
        "da_lq1": nrm(ks[7], (DEPTH, DA_QK_DIM), 0.1),
        "da_lk1": nrm(ks[8], (DEPTH, DA_QK_DIM), 0.1),
        "da_lq2": nrm(ks[9], (DEPTH, DA_QK_DIM), 0.1),
        "da_lk2": nrm(ks[10], (DEPTH, DA_QK_DIM), 0.1),
        "da_subln_g": gain(ks[11], (DEPTH, DA_V_DIM)),
        "rel_bias": nrm(ks[12], (N_BUCKETS, DA_HEADS), 0.5),
        "cv_dw_w": nrm(ks[13], (DEPTH, CV_KERNEL, CV_WIDTH), CV_KERNEL ** -0.5),
        "cv_dw_b": nrm(ks[14], (DEPTH, CV_WIDTH), 0.02),
        "cv_ln_g": gain(ks[15], (DEPTH, CV_WIDTH)),
        "cv_ln_b": nrm(ks[16], (DEPTH, CV_WIDTH), 0.02),
        "ffn_w_up": nrm(ks[17], (DEPTH, D_MODEL, 2 * D_FF), D_MODEL ** -0.5),
        "ffn_conv_w": nrm(ks[18], (DEPTH, FFN_KERNEL, 2 * D_FF), FFN_KERNEL ** -0.5),
        "ffn_conv_b": nrm(ks[19], (DEPTH, 2 * D_FF), 0.02),
        "ffn_w_down": nrm(ks[20], (DEPTH, D_FF, D_MODEL), D_FF ** -0.5),
        "pre_mix_g": gain(ks[21], (DEPTH, D_MODEL)),
        "post_mix_g": gain(ks[22], (DEPTH, D_MODEL)),
        "pre_ffn_g": gain(ks[23], (DEPTH, D_MODEL)),
        "post_ffn_g": gain(ks[24], (DEPTH, D_MODEL)),
    }


def reference(x, w_in, w_out, gm_ln_g, gm_ln_b, gm_w_s, gm_b_s,
              da_lq1, da_lk1, da_lq2, da_lk2, da_subln_g, rel_bias,
              cv_dw_w, cv_dw_b, cv_ln_g, cv_ln_b,
              ffn_w_up, ffn_conv_w, ffn_conv_b, ffn_w_down,
              pre_mix_g, post_mix_g, pre_ffn_g, post_ffn_g):
    B, S, _ = x.shape
    for l in range(DEPTH):
        h = rms_norm(x, pre_mix_g[l])
        p = h @ w_in[l]
        gm_p, q, k, v, cv_p = jnp.split(p, SPLITS, axis=-1)

        gm_u, gm_v = jnp.split(jax.nn.gelu(gm_p), 2, axis=-1)
        out_a = chunked_spatial_gating(gm_u, gm_v, gm_w_s[l], gm_b_s[l], gm_ln_g[l], gm_ln_b[l])

        q = q.reshape(B, S, DA_HEADS, 2, DA_QK_DIM)
        k = k.reshape(B, S, DA_HEADS, 2, DA_QK_DIM)
        v = v.reshape(B, S, DA_HEADS, DA_V_DIM)
        lambda_init = 0.8 - 0.6 * math.exp(-0.3 * l)
        lam = (jnp.exp(jnp.sum(da_lq1[l].astype(jnp.float32) * da_lk1[l].astype(jnp.float32)))
               - jnp.exp(jnp.sum(da_lq2[l].astype(jnp.float32) * da_lk2[l].astype(jnp.float32)))
               + lambda_init)
        attn = differential_attention(q[..., 0, :], q[..., 1, :], k[..., 0, :], k[..., 1, :], v, lam, rel_bias)
        attn = rms_norm(attn, da_subln_g[l]) * (1.0 - lambda_init)
        out_b = attn.reshape(B, S, DA_WIDTH)

        cv_a, cv_g = jnp.split(cv_p, 2, axis=-1)
        out_c = conformer_conv_module(cv_a, cv_g, cv_dw_w[l], cv_dw_b[l], cv_ln_g[l], cv_ln_b[l])

        mix = jnp.concatenate([out_a, out_b, out_c], axis=-1) @ w_out[l]
        x = x + rms_norm(mix, post_mix_g[l])

        h = rms_norm(x, pre_ffn_g[l])
        up = causal_depthwise_conv(h @ ffn_w_up[l], ffn_conv_w[l], ffn_conv_b[l])
        gate, val = jnp.split(up, 2, axis=-1)
        y = (jax.nn.gelu(gate) * val) @ ffn_w_down[l]
        x = x + rms_norm(y, post_ffn_g[l])
    return x
```

```python
import functools
import math

import numpy as np
import jax
import jax.numpy as jnp
from jax import lax
from jax.experimental import pallas as pl
from jax.experimental.pallas import tpu as pltpu

F32 = jnp.float32
BF16 = jnp.bfloat16

D_MODEL = 1024
GM_HEADS = 4
GM_WIDTH = 256
GM_HEAD_DIM = GM_WIDTH // GM_HEADS
CHUNK = 128
DA_HEADS = 4
DA_WIDTH = 512
DA_V_DIM = 128
DA_QK_DIM = 64
CV_GROUPS = 4
CV_WIDTH = 256
CV_KERNEL = 31
D_FF = 2816
FFN_KERNEL = 3
N_BUCKETS = 32
MAX_DISTANCE = 128
EPS = 1e-6
IN_WIDTH = 2 * GM_WIDTH + 3 * DA_WIDTH + 2 * CV_WIDTH

V7X_SUBLANES = 8
V7X_LANES = 128
V7X_MXU_DIM = 256
V7X_SCOPED_VMEM_BYTES = 60000 * 1024

ROW_TILE = 512
ATT_TILE = 512
CV_HALO = 32
FFN_HALO = V7X_SUBLANES
FF_CHUNK = V7X_MXU_DIM

LOG2E = math.log2(math.e)


def _rms(x, g):
    return x * lax.rsqrt(jnp.mean(x * x, axis=-1, keepdims=True) + EPS) * g


def _split_dot(x, g_ref):
    hi = x.astype(BF16)
    lo = (x - hi.astype(F32)).astype(BF16)
    g = g_ref[...]
    return (jnp.dot(hi, g, preferred_element_type=F32)
            + jnp.dot(lo, g, preferred_element_type=F32))


def _mix_in_kernel(x_ref, g_ref, w_ref, lng_ref, lnb_ref, ws_ref, bs_ref,
                   cw_ref, cb_ref, cg_ref, cbeta_ref, gmat_ref,
                   a_ref, q_ref, k_ref, v_ref, c_ref, hbuf):
    tm = x_ref.shape[0]
    i = pl.program_id(0)

    x = x_ref[...]
    hb = _rms(x, g_ref[...]).astype(BF16)

    def proj(c0, n):
        return jnp.dot(hb, w_ref[:, c0:c0 + n], preferred_element_type=F32)

    gm = jax.nn.gelu(proj(0, 2 * GM_WIDTH))
    u = gm[:, :GM_WIDTH]
    v = gm[:, GM_WIDTH:]
    mu = jnp.mean(v, axis=-1, keepdims=True)
    d = v - mu
    var = jnp.mean(d * d, axis=-1, keepdims=True)
    vn = d * lax.rsqrt(var + EPS) * lng_ref[...] + lnb_ref[...]

    row = lax.broadcasted_iota(jnp.int32, (CHUNK, CHUNK), 0)
    col = lax.broadcasted_iota(jnp.int32, (CHUNK, CHUNK), 1)
    causal = row >= col
    wcat = jnp.concatenate(
        [jnp.where(causal, ws_ref[h], 0.0).astype(BF16) for h in range(GM_HEADS)], axis=1)
    head_of_lane = lax.broadcasted_iota(jnp.int32, (CHUNK, GM_WIDTH), 1) // GM_HEAD_DIM
    bs = bs_ref[...]
    for c in range(tm // CHUNK):
        r0 = c * CHUNK
        vc = vn[r0:r0 + CHUNK, :]
        vstack = jnp.concatenate(
            [jnp.where(head_of_lane == h, vc, 0.0).astype(BF16) for h in range(GM_HEADS)], axis=0)
        mixed = jnp.dot(wcat, vstack, preferred_element_type=F32) + bs
        a_ref[r0:r0 + CHUNK, :] = (u[r0:r0 + CHUNK, :] * mixed).astype(BF16)

    c0 = 2 * GM_WIDTH
    q_ref[...] = (proj(c0, DA_WIDTH) * (DA_QK_DIM ** -0.5 * LOG2E)).astype(BF16)
    k_ref[...] = proj(c0 + DA_WIDTH, DA_WIDTH).astype(BF16)
    v_ref[...] = proj(c0 + 2 * DA_WIDTH, DA_WIDTH).astype(BF16)

    cv = proj(c0 + 3 * DA_WIDTH, 2 * CV_WIDTH)
    hg = cv[:, :CV_WIDTH] * jax.nn.sigmoid(cv[:, CV_WIDTH:])

    @pl.when(i == 0)
    def _():
        hbuf[0:CV_HALO, :] = jnp.zeros((CV_HALO, CV_WIDTH), F32)

    hbuf[CV_HALO:CV_HALO + tm, :] = hg
    acc = jnp.broadcast_to(cb_ref[...], (tm, CV_WIDTH))
    for kk in range(CV_KERNEL):
        off = CV_HALO - (CV_KERNEL - 1) + kk
        acc = acc + cw_ref[kk:kk + 1, :] * hbuf[off:off + tm, :]
    hbuf[0:CV_HALO, :] = hbuf[tm:tm + CV_HALO, :]

    gmu = _split_dot(acc, gmat_ref)
    dd = acc - gmu
    gvar = _split_dot(dd * dd, gmat_ref)
    y = dd * lax.rsqrt(gvar + EPS) * cg_ref[...] + cbeta_ref[...]
    c_ref[...] = (y * jax.nn.sigmoid(y)).astype(BF16)


def _mix_in(x, g, w_bf, lng, lnb, ws, bs_full, cw, cb, cg, cbeta, gmat):
    seq = x.shape[0]
    tm = ROW_TILE
    full = lambda shape: pl.BlockSpec(shape, lambda i: (0,) * len(shape))
    rows = lambda n: pl.BlockSpec((tm, n), lambda i: (i, 0))
    out_shapes = (
        jax.ShapeDtypeStruct((seq, GM_WIDTH), BF16),
        jax.ShapeDtypeStruct((seq, DA_WIDTH), BF16),
        jax.ShapeDtypeStruct((seq, DA_WIDTH), BF16),
        jax.ShapeDtypeStruct((seq, DA_WIDTH), BF16),
        jax.ShapeDtypeStruct((seq, CV_WIDTH), BF16),
    )
    return pl.pallas_call(
        _mix_in_kernel,
        out_shape=out_shapes,
        grid=(seq // tm,),
        in_specs=[
            rows(D_MODEL),
            full((1, D_MODEL)),
            full((D_MODEL, IN_WIDTH)),
            full((1, GM_WIDTH)), full((1, GM_WIDTH)),
            full((GM_HEADS, CHUNK, CHUNK)),
            full((CHUNK, GM_WIDTH)),
            full((CV_HALO, CV_WIDTH)),
            full((1, CV_WIDTH)), full((1, CV_WIDTH)), full((1, CV_WIDTH)),
            full((CV_WIDTH, CV_WIDTH)),
        ],
        out_specs=(rows(GM_WIDTH), rows(DA_WIDTH), rows(DA_WIDTH), rows(DA_WIDTH), rows(CV_WIDTH)),
        scratch_shapes=[pltpu.VMEM((CV_HALO + tm, CV_WIDTH), F32)],
        compiler_params=pltpu.CompilerParams(
            dimension_semantics=("arbitrary",),
            vmem_limit_bytes=V7X_SCOPED_VMEM_BYTES),
        name="mix_in",
    )(x, g, w_bf, lng, lnb, ws, bs_full, cw, cb, cg, cbeta, gmat)


def _diff_attn_kernel(q_ref, k_ref, v_ref, bias_ref, lq1_ref, lk1_ref, lq2_ref, lk2_ref, sg_ref,
                      o_ref, qs, acc, m_s, l_s, *, lambda_init):
    t = q_ref.shape[0]
    i = pl.program_id(1)

    q = q_ref[...]
    lane = lax.broadcasted_iota(jnp.int32, (t, 2 * DA_QK_DIM), 1)
    zero = jnp.zeros_like(q)
    qs[0:t, :] = jnp.where(lane < DA_QK_DIM, q, zero)
    qs[t:2 * t, :] = jnp.where(lane >= DA_QK_DIM, q, zero)
    m_s[...] = jnp.full(m_s.shape, -jnp.inf, F32)
    l_s[...] = jnp.zeros(l_s.shape, F32)
    acc[...] = jnp.zeros(acc.shape, F32)

    def step(k0, bias):
        kb = k_ref[pl.ds(k0, t), :]
        vb = v_ref[pl.ds(k0, t), :]
        s = lax.dot_general(qs[...], kb, (((1,), (1,)), ((), ())), preferred_element_type=F32)
        if bias is not None:
            s = s + jnp.concatenate([bias, bias], axis=0)
        m_prev = m_s[...]
        m_new = jnp.maximum(m_prev, jnp.max(s, axis=-1, keepdims=True))
        alpha = jnp.exp2(m_prev - m_new)
        p = jnp.exp2(s - m_new)
        l_s[...] = alpha * l_s[...] + jnp.sum(p, axis=-1, keepdims=True)
        acc[...] = alpha * acc[...] + jnp.dot(p.astype(BF16), vb, preferred_element_type=F32)
        m_s[...] = m_new

    def far_block(j, carry):
        step(pl.multiple_of(j * t, t), None)
        return carry

    lax.fori_loop(0, i - 1, far_block, 0)

    @pl.when(i > 0)
    def _():
        step(pl.multiple_of((i - 1) * t, t), bias_ref[0, :, 0:t])

    step(pl.multiple_of(i * t, t), bias_ref[0, :, t:2 * t])

    o1 = acc[0:t, :] / l_s[0:t, :]
    o2 = acc[t:2 * t, :] / l_s[t:2 * t, :]
    lam = (jnp.exp(jnp.sum(lq1_ref[...] * lk1_ref[...], axis=-1, keepdims=True))
           - jnp.exp(jnp.sum(lq2_ref[...] * lk2_ref[...], axis=-1, keepdims=True))
           + lambda_init)
    o = o1 - lam * o2
    o = _rms(o, sg_ref[...]) * (1.0 - lambda_init)
    o_ref[...] = o.astype(BF16)


def _diff_attn(q, k, v, bias2, lq1, lk1, lq2, lk2, sg, lambda_init):
    seq = q.shape[0]
    t = ATT_TILE
    vec = lambda n: pl.BlockSpec((1, n), lambda h, i: (0, 0))
    return pl.pallas_call(
        functools.partial(_diff_attn_kernel, lambda_init=lambda_init),
        out_shape=jax.ShapeDtypeStruct((seq, DA_WIDTH), BF16),
        grid=(DA_HEADS, seq // t),
        in_specs=[
            pl.BlockSpec((t, DA_V_DIM), lambda h, i: (i, h)),
            pl.BlockSpec((seq, DA_V_DIM), lambda h, i: (0, h)),
            pl.BlockSpec((seq, DA_V_DIM), lambda h, i: (0, h)),
            pl.BlockSpec((1, t, 2 * t), lambda h, i: (h, 0, 0)),
            vec(DA_QK_DIM), vec(DA_QK_DIM), vec(DA_QK_DIM), vec(DA_QK_DIM),
            vec(DA_V_DIM),
        ],
        out_specs=pl.BlockSpec((t, DA_V_DIM), lambda h, i: (i, h)),
        scratch_shapes=[
            pltpu.VMEM((2 * t, 2 * DA_QK_DIM), BF16),
            pltpu.VMEM((2 * t, DA_V_DIM), F32),
            pltpu.VMEM((2 * t, 1), F32),
            pltpu.VMEM((2 * t, 1), F32),
        ],
        compiler_params=pltpu.CompilerParams(
            dimension_semantics=("arbitrary", "arbitrary"),
            vmem_limit_bytes=V7X_SCOPED_VMEM_BYTES),
        name="diff_attn",
    )(q, k, v, bias2, lq1, lk1, lq2, lk2, sg)


def _out_ffn_kernel(x_ref, a_ref, b_ref, c_ref, wo_ref, gpm_ref, gpf_ref, gqf_ref,
                    wu_ref, cw_ref, cb_ref, wd_ref, xo_ref, carry, gbuf, vbuf, yacc):
    tm = x_ref.shape[0]
    i = pl.program_id(0)
    n_chunks = D_FF // FF_CHUNK

    @pl.when(i == 0)
    def _():
        carry[...] = jnp.zeros(carry.shape, F32)

    o_a, o_b, o_c = 0, GM_WIDTH, GM_WIDTH + DA_WIDTH
    mix = (jnp.dot(a_ref[...], wo_ref[o_a:o_b, :], preferred_element_type=F32)
           + jnp.dot(b_ref[...], wo_ref[o_b:o_c, :], preferred_element_type=F32)
           + jnp.dot(c_ref[...], wo_ref[o_c:, :], preferred_element_type=F32))
    x1 = x_ref[...] + _rms(mix, gpm_ref[...])
    hb = _rms(x1, gpf_ref[...]).astype(BF16)

    def conv_up(buf, col):
        up = jnp.dot(hb, wu_ref[:, col:col + FF_CHUNK], preferred_element_type=F32)
        buf[0:FFN_HALO, :] = carry[:, col:col + FF_CHUNK]
        buf[FFN_HALO:FFN_HALO + tm, :] = up
        carry[:, col:col + FF_CHUNK] = up[tm - FFN_HALO:tm, :]
        out = cb_ref[:, col:col + FF_CHUNK] + cw_ref[FFN_KERNEL - 1:FFN_KERNEL, col:col + FF_CHUNK] * up
        for kk in range(FFN_KERNEL - 1):
            off = FFN_HALO - (FFN_KERNEL - 1) + kk
            out = out + cw_ref[kk:kk + 1, col:col + FF_CHUNK] * buf[off:off + tm, :]
        return out

    for ch in range(n_chunks):
        col = ch * FF_CHUNK
        gate = conv_up(gbuf, col)
        val = conv_up(vbuf, D_FF + col)
        act = (jax.nn.gelu(gate) * val).astype(BF16)
        contrib = jnp.dot(act, wd_ref[col:col + FF_CHUNK, :], preferred_element_type=F32)
        if ch == 0:
            yacc[...] = contrib
        else:
            yacc[...] += contrib

    xo_ref[...] = x1 + _rms(yacc[...], gqf_ref[...])


def _out_ffn(x, a, b, c, wo_bf, gpm, gpf, gqf, wu_bf, cw, cb, wd_bf):
    seq = x.shape[0]
    tm = ROW_TILE
    rows = lambda n: pl.BlockSpec((tm, n), lambda i: (i, 0))

    def resident(shape):
        return pl.BlockSpec(shape, lambda i: (0,) * len(shape), pipeline_mode=pl.Buffered(1))

    return pl.pallas_call(
        _out_ffn_kernel,
        out_shape=jax.ShapeDtypeStruct((seq, D_MODEL), F32),
        grid=(seq // tm,),
        in_specs=[
            rows(D_MODEL), rows(GM_WIDTH), rows(DA_WIDTH), rows(CV_WIDTH),
            resident((D_MODEL, D_MODEL)),
            resident((1, D_MODEL)), resident((1, D_MODEL)), resident((1, D_MODEL)),
            resident((D_MODEL, 2 * D_FF)),
            resident((V7X_SUBLANES, 2 * D_FF)),
            resident((1, 2 * D_FF)),
            resident((D_FF, D_MODEL)),
        ],
        out_specs=rows(D_MODEL),
        scratch_shapes=[
            pltpu.VMEM((FFN_HALO, 2 * D_FF), F32),
            pltpu.VMEM((FFN_HALO + tm, FF_CHUNK), F32),
            pltpu.VMEM((FFN_HALO + tm, FF_CHUNK), F32),
            pltpu.VMEM((tm, D_MODEL), F32),
        ],
        compiler_params=pltpu.CompilerParams(
            dimension_semantics=("arbitrary",),
            vmem_limit_bytes=V7X_SCOPED_VMEM_BYTES),
        name="out_ffn",
    )(x, a, b, c, wo_bf, gpm, gpf, gqf, wu_bf, cw, cb, wd_bf)


def _bucket_table():
    n = np.arange(MAX_DISTANCE + 1)
    max_exact = N_BUCKETS // 2
    nf = np.maximum(n, 1).astype(np.float32)
    large = max_exact + (np.log(nf / max_exact) / math.log(MAX_DISTANCE / max_exact)
                         * (N_BUCKETS - max_exact)).astype(np.int32)
    large = np.minimum(large, N_BUCKETS - 1)
    return np.where(n < max_exact, n, large).astype(np.int32)


def _bias_tiles(rel_bias):
    t = ATT_TILE
    rel = (np.arange(t)[:, None] + t) - np.arange(2 * t)[None, :]
    bucket = _bucket_table()[np.clip(rel, 0, MAX_DISTANCE)]
    table = (rel_bias - rel_bias[N_BUCKETS - 1][None, :]) * LOG2E
    vals = jnp.transpose(table[bucket], (2, 0, 1))
    return jnp.where(jnp.asarray(rel >= 0)[None], vals, -jnp.inf).astype(F32)


def _group_mean_matrix():
    g = np.arange(CV_WIDTH) // (CV_WIDTH // CV_GROUPS)
    return jnp.asarray((g[:, None] == g[None, :]) / (CV_WIDTH // CV_GROUPS), dtype=BF16)


def _pad_rows(w, rows):
    return jnp.concatenate([w, jnp.zeros((rows - w.shape[0], w.shape[1]), w.dtype)], axis=0)


def kernel(x, w_in, w_out, gm_ln_g, gm_ln_b, gm_w_s, gm_b_s, da_lq1, da_lk1, da_lq2, da_lk2, da_subln_g, rel_bias, cv_dw_w, cv_dw_b, cv_ln_g, cv_ln_b, ffn_w_up, ffn_conv_w, ffn_conv_b, ffn_w_down, pre_mix_g, post_mix_g, pre_ffn_g, post_ffn_g):
    batch, seq, d_model = x.shape
    depth = w_in.shape[0]
    assert batch == 1 and d_model == D_MODEL and w_in.shape[2] == IN_WIDTH
    assert seq % ROW_TILE == 0 and seq % ATT_TILE == 0 and ATT_TILE >= MAX_DISTANCE
    assert ROW_TILE % CHUNK == 0 and D_FF % FF_CHUNK == 0 and CV_HALO >= CV_KERNEL - 1

    bias2 = _bias_tiles(rel_bias)
    gmat = _group_mean_matrix()
    row = lambda p: p.reshape(1, -1)

    xs = x[0]
    for l in range(depth):
        lambda_init = 0.8 - 0.6 * math.exp(-0.3 * l)
        bs_full = jnp.repeat(jnp.transpose(gm_b_s[l]), GM_HEAD_DIM, axis=1)
        a, q, k, v, c = _mix_in(
            xs, row(pre_mix_g[l]), w_in[l].astype(BF16),
            row(gm_ln_g[l]), row(gm_ln_b[l]), gm_w_s[l], bs_full,
            _pad_rows(cv_dw_w[l], CV_HALO), row(cv_dw_b[l]), row(cv_ln_g[l]), row(cv_ln_b[l]), gmat)
        b = _diff_attn(q, k, v, bias2, row(da_lq1[l]), row(da_lk1[l]), row(da_lq2[l]), row(da_lk2[l]),
                       row(da_subln_g[l]), lambda_init)
        xs = _out_ffn(
            xs, a, b, c, w_out[l].astype(BF16),
            row(post_mix_g[l]), row(pre_ffn_g[l]), row(post_ffn_g[l]),
            ffn_w_up[l].astype(BF16), _pad_rows(ffn_conv_w[l], V7X_SUBLANES), row(ffn_conv_b[l]),
            ffn_w_down[l].astype(BF16))
    return xs[None]
```

```python
import functools
import math

import numpy as np
import jax
import jax.numpy as jnp
from jax import lax
from jax.experimental import pallas as pl
from jax.experimental.pallas import tpu as pltpu

F32 = jnp.float32
BF16 = jnp.bfloat16

D_MODEL = 1024
GM_HEADS = 4
GM_WIDTH = 256
GM_HEAD_DIM = GM_WIDTH // GM_HEADS
CHUNK = 128
DA_HEADS = 4
DA_WIDTH = 512
DA_V_DIM = 128
DA_QK_DIM = 64
CV_GROUPS = 4
CV_WIDTH = 256
CV_KERNEL = 31
D_FF = 2816
FFN_KERNEL = 3
N_BUCKETS = 32
MAX_DISTANCE = 128
EPS = 1e-6
IN_WIDTH = 2 * GM_WIDTH + 3 * DA_WIDTH + 2 * CV_WIDTH

V7X_SUBLANES = 8
V7X_LANES = 128
V7X_MXU_DIM = 256
V7X_SCOPED_VMEM_BYTES = 60000 * 1024

ROW_TILE = 512
ATT_TILE = 512
CV_HALO = 32
FFN_HALO = V7X_SUBLANES
FF_CHUNK = V7X_MXU_DIM

LOG2E = math.log2(math.e)


def _rms(x, g):
    return x * lax.rsqrt(jnp.mean(x * x, axis=-1, keepdims=True) + EPS) * g


def _split_dot(x, g_ref):
    hi = x.astype(BF16)
    lo = (x - hi.astype(F32)).astype(BF16)
    g = g_ref[...]
    return (jnp.dot(hi, g, preferred_element_type=F32)
            + jnp.dot(lo, g, preferred_element_type=F32))


def _mix_in_kernel(x_ref, g_ref, w_ref, lng_ref, lnb_ref, ws_ref, bs_ref,
                   cw_ref, cb_ref, cg_ref, cbeta_ref, gmat_ref,
                   a_ref, q_ref, k_ref, v_ref, c_ref, hbuf):
    tm = x_ref.shape[0]
    i = pl.program_id(0)

    x = x_ref[...]
    hb = _rms(x, g_ref[...]).astype(BF16)

    def proj(c0, n):
        return jnp.dot(hb, w_ref[:, c0:c0 + n], preferred_element_type=F32)

    gm = jax.nn.gelu(proj(0, 2 * GM_WIDTH))
    u = gm[:, :GM_WIDTH]
    v = gm[:, GM_WIDTH:]
    mu = jnp.mean(v, axis=-1, keepdims=True)
    d = v - mu
    var = jnp.mean(d * d, axis=-1, keepdims=True)
    vn = d * lax.rsqrt(var + EPS) * lng_ref[...] + lnb_ref[...]

    row = lax.broadcasted_iota(jnp.int32, (CHUNK, CHUNK), 0)
    col = lax.broadcasted_iota(jnp.int32, (CHUNK, CHUNK), 1)
    causal = row >= col
    wcat = jnp.concatenate(
        [jnp.where(causal, ws_ref[h], 0.0).astype(BF16) for h in range(GM_HEADS)], axis=1)
    head_of_lane = lax.broadcasted_iota(jnp.int32, (CHUNK, GM_WIDTH), 1) // GM_HEAD_DIM
    bs = bs_ref[...]
    for c in range(tm // CHUNK):
        r0 = c * CHUNK
        vc = vn[r0:r0 + CHUNK, :]
        vstack = jnp.concatenate(
            [jnp.where(head_of_lane == h, vc, 0.0).astype(BF16) for h in range(GM_HEADS)], axis=0)
        mixed = jnp.dot(wcat, vstack, preferred_element_type=F32) + bs
        a_ref[r0:r0 + CHUNK, :] = (u[r0:r0 + CHUNK, :] * mixed).astype(BF16)

    c0 = 2 * GM_WIDTH
    q_ref[...] = (proj(c0, DA_WIDTH) * (DA_QK_DIM ** -0.5 * LOG2E)).astype(BF16)
    k_ref[...] = proj(c0 + DA_WIDTH, DA_WIDTH).astype(BF16)
    v_ref[...] = proj(c0 + 2 * DA_WIDTH, DA_WIDTH).astype(BF16)

    cv = proj(c0 + 3 * DA_WIDTH, 2 * CV_WIDTH)
    hg = cv[:, :CV_WIDTH] * jax.nn.sigmoid(cv[:, CV_WIDTH:])

    @pl.when(i == 0)
    def _():
        hbuf[0:CV_HALO, :] = jnp.zeros((CV_HALO, CV_WIDTH), F32)

    hbuf[CV_HALO:CV_HALO + tm, :] = hg
    acc = jnp.broadcast_to(cb_ref[...], (tm, CV_WIDTH))
    for kk in range(CV_KERNEL):
        off = CV_HALO - (CV_KERNEL - 1) + kk
        acc = acc + cw_ref[kk:kk + 1, :] * hbuf[off:off + tm, :]
    hbuf[0:CV_HALO, :] = hbuf[tm:tm + CV_HALO, :]

    gmu = _split_dot(acc, gmat_ref)
    dd = acc - gmu
    gvar = _split_dot(dd * dd, gmat_ref)
    y = dd * lax.rsqrt(gvar + EPS) * cg_ref[...] + cbeta_ref[...]
    c_ref[...] = (y * jax.nn.sigmoid(y)).astype(BF16)


def _mix_in(x, g, w_bf, lng, lnb, ws, bs_full, cw, cb, cg, cbeta, gmat):
    seq = x.shape[0]
    tm = ROW_TILE
    full = lambda shape: pl.BlockSpec(shape, lambda i: (0,) * len(shape))
    rows = lambda n: pl.BlockSpec((tm, n), lambda i: (i, 0))
    out_shapes = (
        jax.ShapeDtypeStruct((seq, GM_WIDTH), BF16),
        jax.ShapeDtypeStruct((seq, DA_WIDTH), BF16),
        jax.ShapeDtypeStruct((seq, DA_WIDTH), BF16),
        jax.ShapeDtypeStruct((seq, DA_WIDTH), BF16),
        jax.ShapeDtypeStruct((seq, CV_WIDTH), BF16),
    )
    return pl.pallas_call(
        _mix_in_kernel,
        out_shape=out_shapes,
        grid=(seq // tm,),
        in_specs=[
            rows(D_MODEL),
            full((1, D_MODEL)),
            full((D_MODEL, IN_WIDTH)),
            full((1, GM_WIDTH)), full((1, GM_WIDTH)),
            full((GM_HEADS, CHUNK, CHUNK)),
            full((CHUNK, GM_WIDTH)),
            full((CV_HALO, CV_WIDTH)),
            full((1, CV_WIDTH)), full((1, CV_WIDTH)), full((1, CV_WIDTH)),
            full((CV_WIDTH, CV_WIDTH)),
        ],
        out_specs=(rows(GM_WIDTH), rows(DA_WIDTH), rows(DA_WIDTH), rows(DA_WIDTH), rows(CV_WIDTH)),
        scratch_shapes=[pltpu.VMEM((CV_HALO + tm, CV_WIDTH), F32)],
        compiler_params=pltpu.CompilerParams(
            dimension_semantics=("arbitrary",),
            vmem_limit_bytes=V7X_SCOPED_VMEM_BYTES),
        name="mix_in",
    )(x, g, w_bf, lng, lnb, ws, bs_full, cw, cb, cg, cbeta, gmat)


def _diff_attn_kernel(q_ref, k_ref, v_ref, bias_ref, lq1_ref, lk1_ref, lq2_ref, lk2_ref, sg_ref,
                      o_ref, qs, acc, m_s, l_s, *, lambda_init):
    t = q_ref.shape[0]
    i = pl.program_id(1)

    q = q_ref[...]
    lane = lax.broadcasted_iota(jnp.int32, (t, 2 * DA_QK_DIM), 1)
    zero = jnp.zeros_like(q)
    qs[0:t, :] = jnp.where(lane < DA_QK_DIM, q, zero)
    qs[t:2 * t, :] = jnp.where(lane >= DA_QK_DIM, q, zero)
    m_s[...] = jnp.full(m_s.shape, -jnp.inf, F32)
    l_s[...] = jnp.zeros(l_s.shape, F32)
    acc[...] = jnp.zeros(acc.shape, F32)
    n_col = t // V7X_LANES

    def step(k0, bias):
        kb = k_ref[pl.ds(k0, t), :]
        vb = v_ref[pl.ds(k0, t), :]
        s = lax.dot_general(qs[...], kb, (((1,), (1,)), ((), ())), preferred_element_type=F32)
        cols = [s[:, c * V7X_LANES:(c + 1) * V7X_LANES] for c in range(n_col)]
        if bias is not None:
            cols = [sc + jnp.concatenate([bias[:, c * V7X_LANES:(c + 1) * V7X_LANES]] * 2, axis=0)
                    for c, sc in enumerate(cols)]
        lane_max = functools.reduce(jnp.maximum, cols)
        m_prev = m_s[...]
        m_new = jnp.maximum(m_prev, jnp.max(lane_max, axis=-1, keepdims=True))
        alpha = jnp.exp2(m_prev - m_new)
        ps = [jnp.exp2(sc - m_new) for sc in cols]
        l_s[...] = alpha * l_s[...] + functools.reduce(jnp.add, ps)
        p = jnp.concatenate([pc.astype(BF16) for pc in ps], axis=1)
        acc[...] = alpha * acc[...] + jnp.dot(p, vb, preferred_element_type=F32)
        m_s[...] = m_new

    def far_block(j, carry):
        step(pl.multiple_of(j * t, t), None)
        return carry

    lax.fori_loop(0, i - 1, far_block, 0)

    @pl.when(i > 0)
    def _():
        step(pl.multiple_of((i - 1) * t, t), bias_ref[0, :, 0:t])

    step(pl.multiple_of(i * t, t), bias_ref[0, :, t:2 * t])

    o1 = acc[0:t, :] / jnp.sum(l_s[0:t, :], axis=-1, keepdims=True)
    o2 = acc[t:2 * t, :] / jnp.sum(l_s[t:2 * t, :], axis=-1, keepdims=True)
    lam = (jnp.exp(jnp.sum(lq1_ref[...] * lk1_ref[...], axis=-1, keepdims=True))
           - jnp.exp(jnp.sum(lq2_ref[...] * lk2_ref[...], axis=-1, keepdims=True))
           + lambda_init)
    o = o1 - lam * o2
    o = _rms(o, sg_ref[...]) * (1.0 - lambda_init)
    o_ref[...] = o.astype(BF16)


def _diff_attn(q, k, v, bias2, lq1, lk1, lq2, lk2, sg, lambda_init):
    seq = q.shape[0]
    t = ATT_TILE
    vec = lambda n: pl.BlockSpec((1, n), lambda h, i: (0, 0))
    return pl.pallas_call(
        functools.partial(_diff_attn_kernel, lambda_init=lambda_init),
        out_shape=jax.ShapeDtypeStruct((seq, DA_WIDTH), BF16),
        grid=(DA_HEADS, seq // t),
        in_specs=[
            pl.BlockSpec((t, DA_V_DIM), lambda h, i: (i, h)),
            pl.BlockSpec((seq, DA_V_DIM), lambda h, i: (0, h)),
            pl.BlockSpec((seq, DA_V_DIM), lambda h, i: (0, h)),
            pl.BlockSpec((1, t, 2 * t), lambda h, i: (h, 0, 0)),
            vec(DA_QK_DIM), vec(DA_QK_DIM), vec(DA_QK_DIM), vec(DA_QK_DIM),
            vec(DA_V_DIM),
        ],
        out_specs=pl.BlockSpec((t, DA_V_DIM), lambda h, i: (i, h)),
        scratch_shapes=[
            pltpu.VMEM((2 * t, 2 * DA_QK_DIM), BF16),
            pltpu.VMEM((2 * t, DA_V_DIM), F32),
            pltpu.VMEM((2 * t, V7X_LANES), F32),
            pltpu.VMEM((2 * t, V7X_LANES), F32),
        ],
        compiler_params=pltpu.CompilerParams(
            dimension_semantics=("arbitrary", "arbitrary"),
            vmem_limit_bytes=V7X_SCOPED_VMEM_BYTES),
        name="diff_attn",
    )(q, k, v, bias2, lq1, lk1, lq2, lk2, sg)


def _out_ffn_kernel(x_ref, a_ref, b_ref, c_ref, wo_ref, gpm_ref, gpf_ref, gqf_ref,
                    wu_ref, cw_ref, cb_ref, wd_ref, xo_ref, carry, gbuf, vbuf, yacc):
    tm = x_ref.shape[0]
    i = pl.program_id(0)
    n_chunks = D_FF // FF_CHUNK

    @pl.when(i == 0)
    def _():
        carry[...] = jnp.zeros(carry.shape, F32)

    o_a, o_b, o_c = 0, GM_WIDTH, GM_WIDTH + DA_WIDTH
    mix = (jnp.dot(a_ref[...], wo_ref[o_a:o_b, :], preferred_element_type=F32)
           + jnp.dot(b_ref[...], wo_ref[o_b:o_c, :], preferred_element_type=F32)
           + jnp.dot(c_ref[...], wo_ref[o_c:, :], preferred_element_type=F32))
    x1 = x_ref[...] + _rms(mix, gpm_ref[...])
    hb = _rms(x1, gpf_ref[...]).astype(BF16)

    def conv_up(buf, col):
        up = jnp.dot(hb, wu_ref[:, col:col + FF_CHUNK], preferred_element_type=F32)
        buf[0:FFN_HALO, :] = carry[:, col:col + FF_CHUNK]
        buf[FFN_HALO:FFN_HALO + tm, :] = up
        carry[:, col:col + FF_CHUNK] = up[tm - FFN_HALO:tm, :]
        out = cb_ref[:, col:col + FF_CHUNK] + cw_ref[FFN_KERNEL - 1:FFN_KERNEL, col:col + FF_CHUNK] * up
        for kk in range(FFN_KERNEL - 1):
            off = FFN_HALO - (FFN_KERNEL - 1) + kk
            out = out + cw_ref[kk:kk + 1, col:col + FF_CHUNK] * buf[off:off + tm, :]
        return out

    for ch in range(n_chunks):
        col = ch * FF_CHUNK
        gate = conv_up(gbuf, col)
        val = conv_up(vbuf, D_FF + col)
        act = (jax.nn.gelu(gate) * val).astype(BF16)
        contrib = jnp.dot(act, wd_ref[col:col + FF_CHUNK, :], preferred_element_type=F32)
        if ch == 0:
            yacc[...] = contrib
        else:
            yacc[...] += contrib

    xo_ref[...] = x1 + _rms(yacc[...], gqf_ref[...])


def _out_ffn(x, a, b, c, wo_bf, gpm, gpf, gqf, wu_bf, cw, cb, wd_bf):
    seq = x.shape[0]
    tm = ROW_TILE
    rows = lambda n: pl.BlockSpec((tm, n), lambda i: (i, 0))

    def resident(shape):
        return pl.BlockSpec(shape, lambda i: (0,) * len(shape), pipeline_mode=pl.Buffered(1))

    return pl.pallas_call(
        _out_ffn_kernel,
        out_shape=jax.ShapeDtypeStruct((seq, D_MODEL), F32),
        grid=(seq // tm,),
        in_specs=[
            rows(D_MODEL), rows(GM_WIDTH), rows(DA_WIDTH), rows(CV_WIDTH),
            resident((D_MODEL, D_MODEL)),
            resident((1, D_MODEL)), resident((1, D_MODEL)), resident((1, D_MODEL)),
            resident((D_MODEL, 2 * D_FF)),
            resident((V7X_SUBLANES, 2 * D_FF)),
            resident((1, 2 * D_FF)),
            resident((D_FF, D_MODEL)),
        ],
        out_specs=rows(D_MODEL),
        scratch_shapes=[
            pltpu.VMEM((FFN_HALO, 2 * D_FF), F32),
            pltpu.VMEM((FFN_HALO + tm, FF_CHUNK), F32),
            pltpu.VMEM((FFN_HALO + tm, FF_CHUNK), F32),
            pltpu.VMEM((tm, D_MODEL), F32),
        ],
        compiler_params=pltpu.CompilerParams(
            dimension_semantics=("arbitrary",),
            vmem_limit_bytes=V7X_SCOPED_VMEM_BYTES),
        name="out_ffn",
    )(x, a, b, c, wo_bf, gpm, gpf, gqf, wu_bf, cw, cb, wd_bf)


def _bucket_table():
    n = np.arange(MAX_DISTANCE + 1)
    max_exact = N_BUCKETS // 2
    nf = np.maximum(n, 1).astype(np.float32)
    large = max_exact + (np.log(nf / max_exact) / math.log(MAX_DISTANCE / max_exact)
                         * (N_BUCKETS - max_exact)).astype(np.int32)
    large = np.minimum(large, N_BUCKETS - 1)
    return np.where(n < max_exact, n, large).astype(np.int32)


def _bias_tiles(rel_bias):
    t = ATT_TILE
    heads = rel_bias.shape[1]
    table = jnp.transpose((rel_bias - rel_bias[N_BUCKETS - 1][None, :]) * LOG2E)
    near = table[:, _bucket_table()[:MAX_DISTANCE]]
    width = 3 * t
    n_far = 2 * t - MAX_DISTANCE
    w = jnp.concatenate([jnp.zeros((heads, n_far), F32), near[:, ::-1].astype(F32),
                         jnp.full((heads, width - n_far - MAX_DISTANCE), -jnp.inf, F32)], axis=1)
    hankel = jnp.tile(w, (1, t + 1))[:, :t * (width + 1)].reshape(heads, t, width + 1)[:, :, :2 * t]
    return hankel[:, ::-1, :]


def _group_mean_matrix():
    g = np.arange(CV_WIDTH) // (CV_WIDTH // CV_GROUPS)
    return jnp.asarray((g[:, None] == g[None, :]) / (CV_WIDTH // CV_GROUPS), dtype=BF16)


def _pad_rows(w, rows):
    return jnp.concatenate([w, jnp.zeros((rows - w.shape[0], w.shape[1]), w.dtype)], axis=0)


def kernel(x, w_in, w_out, gm_ln_g, gm_ln_b, gm_w_s, gm_b_s, da_lq1, da_lk1, da_lq2, da_lk2, da_subln_g, rel_bias, cv_dw_w, cv_dw_b, cv_ln_g, cv_ln_b, ffn_w_up, ffn_conv_w, ffn_conv_b, ffn_w_down, pre_mix_g, post_mix_g, pre_ffn_g, post_ffn_g):
    batch, seq, d_model = x.shape
    depth = w_in.shape[0]
    assert batch == 1 and d_model == D_MODEL and w_in.shape[2] == IN_WIDTH
    assert seq % ROW_TILE == 0 and seq % ATT_TILE == 0 and ATT_TILE >= MAX_DISTANCE
    assert ROW_TILE % CHUNK == 0 and D_FF % FF_CHUNK == 0 and CV_HALO >= CV_KERNEL - 1

    bias2 = _bias_tiles(rel_bias)
    gmat = _group_mean_matrix()
    row = lambda p: p.reshape(1, -1)

    xs = x[0]
    for l in range(depth):
        lambda_init = 0.8 - 0.6 * math.exp(-0.3 * l)
        bs_full = jnp.repeat(jnp.transpose(gm_b_s[l]), GM_HEAD_DIM, axis=1)
        a, q, k, v, c = _mix_in(
            xs, row(pre_mix_g[l]), w_in[l].astype(BF16),
            row(gm_ln_g[l]), row(gm_ln_b[l]), gm_w_s[l], bs_full,
            _pad_rows(cv_dw_w[l], CV_HALO), row(cv_dw_b[l]), row(cv_ln_g[l]), row(cv_ln_b[l]), gmat)
        b = _diff_attn(q, k, v, bias2, row(da_lq1[l]), row(da_lk1[l]), row(da_lq2[l]), row(da_lk2[l]),
                       row(da_subln_g[l]), lambda_init)
        xs = _out_ffn(
            xs, a, b, c, w_out[l].astype(BF16),
            row(post_mix_g[l]), row(pre_ffn_g[l]), row(post_ffn_g[l]),
            ffn_w_up[l].astype(BF16), _pad_rows(ffn_conv_w[l], V7X_SUBLANES), row(ffn_conv_b[l]),
            ffn_w_down[l].astype(BF16))
    return xs[None]
```

```python
import functools
import math

import numpy as np
import jax
import jax.numpy as jnp
from jax import lax
from jax.experimental import pallas as pl
from jax.experimental.pallas import tpu as pltpu

F32 = jnp.float32
BF16 = jnp.bfloat16

D_MODEL = 1024
GM_HEADS = 4
GM_WIDTH = 256
GM_HEAD_DIM = GM_WIDTH // GM_HEADS
CHUNK = 128
DA_HEADS = 4
DA_WIDTH = 512
DA_V_DIM = 128
DA_QK_DIM = 64
CV_GROUPS = 4
CV_WIDTH = 256
CV_KERNEL = 31
D_FF = 2816
FFN_KERNEL = 3
N_BUCKETS = 32
MAX_DISTANCE = 128
EPS = 1e-6
IN_WIDTH = 2 * GM_WIDTH + 3 * DA_WIDTH + 2 * CV_WIDTH

V7X_SUBLANES = 8
V7X_LANES = 128
V7X_MXU_DIM = 256
V7X_SCOPED_VMEM_BYTES = 60000 * 1024

ROW_TILE = 512
ATT_TILE = 512
ATT_GROUP = V7X_MXU_DIM
CV_HALO = 32
FFN_HALO = V7X_SUBLANES
FF_CHUNK = V7X_MXU_DIM

LOG2E = math.log2(math.e)


def _rms(x, g):
    return x * lax.rsqrt(jnp.mean(x * x, axis=-1, keepdims=True) + EPS) * g


def _split_dot(x, g_ref):
    hi = x.astype(BF16)
    lo = (x - hi.astype(F32)).astype(BF16)
    g = g_ref[...]
    return (jnp.dot(hi, g, preferred_element_type=F32)
            + jnp.dot(lo, g, preferred_element_type=F32))


def _mix_in_kernel(x_ref, g_ref, w_ref, wqt_ref, wvt_ref, lng_ref, lnb_ref, ws_ref, bs_ref,
                   cw_ref, cb_ref, cg_ref, cbeta_ref, gmat_ref,
                   a_ref, qt_ref, k_ref, vt_ref, c_ref, hbuf):
    tm = x_ref.shape[0]
    i = pl.program_id(0)

    x = x_ref[...]
    hb = _rms(x, g_ref[...]).astype(BF16)

    def proj(c0, n):
        return jnp.dot(hb, w_ref[:, c0:c0 + n], preferred_element_type=F32)

    gm = jax.nn.gelu(proj(0, 2 * GM_WIDTH))
    u = gm[:, :GM_WIDTH]
    v = gm[:, GM_WIDTH:]
    mu = jnp.mean(v, axis=-1, keepdims=True)
    d = v - mu
    var = jnp.mean(d * d, axis=-1, keepdims=True)
    vn = d * lax.rsqrt(var + EPS) * lng_ref[...] + lnb_ref[...]

    row = lax.broadcasted_iota(jnp.int32, (CHUNK, CHUNK), 0)
    col = lax.broadcasted_iota(jnp.int32, (CHUNK, CHUNK), 1)
    causal = row >= col
    wcat = jnp.concatenate(
        [jnp.where(causal, ws_ref[h], 0.0).astype(BF16) for h in range(GM_HEADS)], axis=1)
    head_of_lane = lax.broadcasted_iota(jnp.int32, (CHUNK, GM_WIDTH), 1) // GM_HEAD_DIM
    bs = bs_ref[...]
    for c in range(tm // CHUNK):
        r0 = c * CHUNK
        vc = vn[r0:r0 + CHUNK, :]
        vstack = jnp.concatenate(
            [jnp.where(head_of_lane == h, vc, 0.0).astype(BF16) for h in range(GM_HEADS)], axis=0)
        mixed = jnp.dot(wcat, vstack, preferred_element_type=F32) + bs
        a_ref[r0:r0 + CHUNK, :] = (u[r0:r0 + CHUNK, :] * mixed).astype(BF16)

    c0 = 2 * GM_WIDTH
    def proj_t(wt_ref):
        return lax.dot_general(wt_ref[...], hb, (((1,), (1,)), ((), ())), preferred_element_type=F32)

    qt = (proj_t(wqt_ref) * (DA_QK_DIM ** -0.5 * LOG2E)).astype(BF16)
    qt_ref[:, 0, :, :] = qt.reshape(DA_HEADS, DA_V_DIM, tm)
    k_ref[...] = proj(c0 + DA_WIDTH, DA_WIDTH).astype(BF16)
    vt_ref[:, 0, :, :] = proj_t(wvt_ref).astype(BF16).reshape(DA_HEADS, DA_V_DIM, tm)

    cv = proj(c0 + 3 * DA_WIDTH, 2 * CV_WIDTH)
    hg = cv[:, :CV_WIDTH] * jax.nn.sigmoid(cv[:, CV_WIDTH:])

    @pl.when(i == 0)
    def _():
        hbuf[0:CV_HALO, :] = jnp.zeros((CV_HALO, CV_WIDTH), F32)

    hbuf[CV_HALO:CV_HALO + tm, :] = hg
    acc = jnp.broadcast_to(cb_ref[...], (tm, CV_WIDTH))
    for kk in range(CV_KERNEL):
        off = CV_HALO - (CV_KERNEL - 1) + kk
        acc = acc + cw_ref[kk:kk + 1, :] * hbuf[off:off + tm, :]
    hbuf[0:CV_HALO, :] = hbuf[tm:tm + CV_HALO, :]

    gmu = _split_dot(acc, gmat_ref)
    dd = acc - gmu
    gvar = _split_dot(dd * dd, gmat_ref)
    y = dd * lax.rsqrt(gvar + EPS) * cg_ref[...] + cbeta_ref[...]
    c_ref[...] = (y * jax.nn.sigmoid(y)).astype(BF16)


def _mix_in(x, g, w_bf, wqt_bf, wvt_bf, lng, lnb, ws, bs_full, cw, cb, cg, cbeta, gmat):
    seq = x.shape[0]
    tm = ROW_TILE
    full = lambda shape: pl.BlockSpec(shape, lambda i: (0,) * len(shape))
    rows = lambda n: pl.BlockSpec((tm, n), lambda i: (i, 0))
    slabs = pl.BlockSpec((DA_HEADS, 1, DA_V_DIM, tm), lambda i: (0, i, 0, 0))
    slab_shape = jax.ShapeDtypeStruct((DA_HEADS, seq // tm, DA_V_DIM, tm), BF16)
    out_shapes = (
        jax.ShapeDtypeStruct((seq, GM_WIDTH), BF16),
        slab_shape,
        jax.ShapeDtypeStruct((seq, DA_WIDTH), BF16),
        slab_shape,
        jax.ShapeDtypeStruct((seq, CV_WIDTH), BF16),
    )
    return pl.pallas_call(
        _mix_in_kernel,
        out_shape=out_shapes,
        grid=(seq // tm,),
        in_specs=[
            rows(D_MODEL),
            full((1, D_MODEL)),
            full((D_MODEL, IN_WIDTH)),
            full((DA_WIDTH, D_MODEL)), full((DA_WIDTH, D_MODEL)),
            full((1, GM_WIDTH)), full((1, GM_WIDTH)),
            full((GM_HEADS, CHUNK, CHUNK)),
            full((CHUNK, GM_WIDTH)),
            full((CV_HALO, CV_WIDTH)),
            full((1, CV_WIDTH)), full((1, CV_WIDTH)), full((1, CV_WIDTH)),
            full((CV_WIDTH, CV_WIDTH)),
        ],
        out_specs=(rows(GM_WIDTH), slabs, rows(DA_WIDTH), slabs, rows(CV_WIDTH)),
        scratch_shapes=[pltpu.VMEM((CV_HALO + tm, CV_WIDTH), F32)],
        compiler_params=pltpu.CompilerParams(
            dimension_semantics=("arbitrary",),
            vmem_limit_bytes=V7X_SCOPED_VMEM_BYTES),
        name="mix_in",
    )(x, g, w_bf, wqt_bf, wvt_bf, lng, lnb, ws, bs_full, cw, cb, cg, cbeta, gmat)


def _diff_attn_kernel(qt_ref, k_ref, vt_ref, bias_ref, lq1_ref, lk1_ref, lq2_ref, lk2_ref, sg_ref,
                      o_ref, qst, acc, m_s, l_s, s_buf, *, lambda_init):
    t = o_ref.shape[0]
    i = pl.program_id(1)

    qt = qt_ref[0, 0]
    chan = lax.broadcasted_iota(jnp.int32, qt.shape, 0)
    zero = jnp.zeros_like(qt)
    qst[:, 0:t] = jnp.where(chan < DA_QK_DIM, qt, zero)
    qst[:, t:2 * t] = jnp.where(chan >= DA_QK_DIM, qt, zero)
    m_s[...] = jnp.full(m_s.shape, -jnp.inf, F32)
    l_s[...] = jnp.zeros(l_s.shape, F32)
    acc[...] = jnp.zeros(acc.shape, F32)

    n_groups = 2 * t // ATT_GROUP
    cols = [slice(g * ATT_GROUP, (g + 1) * ATT_GROUP) for g in range(n_groups)]

    def scores(j, slot):
        kb = k_ref[pl.ds(pl.multiple_of(j * t, t), t), :]
        for g in range(n_groups):
            s_buf[slot, :, cols[g]] = jnp.dot(kb, qst[:, cols[g]], preferred_element_type=F32)

    def softmax_pv(j, slot, bias):
        for g in range(n_groups):
            s = s_buf[slot, :, cols[g]]
            if bias is not None:
                b0 = (g * ATT_GROUP) % t
                s = s + bias[:, b0:b0 + ATT_GROUP]
            m_prev = m_s[:, cols[g]]
            m_new = jnp.maximum(m_prev, jnp.max(s, axis=0, keepdims=True))
            alpha = jnp.exp2(m_prev - m_new)
            p = jnp.exp2(s - m_new)
            l_s[:, cols[g]] = alpha * l_s[:, cols[g]] + jnp.sum(p, axis=0, keepdims=True)
            acc[:, cols[g]] = (alpha * acc[:, cols[g]]
                               + jnp.dot(vt_ref[0, j], p.astype(BF16), preferred_element_type=F32))
            m_s[:, cols[g]] = m_new

    n_far = i - 1
    odd = jnp.maximum(n_far, 0) % 2
    scores(0, 0)

    @pl.when(odd == 1)
    def _():
        softmax_pv(0, 0, None)
        scores(1, 0)

    def far_pair(pp, carry):
        j = odd + 2 * pp
        scores(j + 1, 1)
        softmax_pv(j, 0, None)
        scores(j + 2, 0)
        softmax_pv(j + 1, 1, None)
        return carry

    lax.fori_loop(0, jnp.maximum(n_far, 0) // 2, far_pair, 0)

    @pl.when(i > 0)
    def _():
        scores(i, 1)
        softmax_pv(i - 1, 0, bias_ref[0, 0:t, :])
        softmax_pv(i, 1, bias_ref[0, t:2 * t, :])

    @pl.when(i == 0)
    def _():
        softmax_pv(0, 0, bias_ref[0, t:2 * t, :])

    o1 = acc[:, 0:t] / l_s[:, 0:t]
    o2 = acc[:, t:2 * t] / l_s[:, t:2 * t]
    lam = (jnp.exp(jnp.sum(lq1_ref[...] * lk1_ref[...], axis=-1, keepdims=True))
           - jnp.exp(jnp.sum(lq2_ref[...] * lk2_ref[...], axis=-1, keepdims=True))
           + lambda_init)
    ot = o1 - lam * o2
    ot = ot * lax.rsqrt(jnp.mean(ot * ot, axis=0, keepdims=True) + EPS)
    o_ref[...] = (jnp.transpose(ot) * sg_ref[...] * (1.0 - lambda_init)).astype(BF16)


def _diff_attn(qt, k, vt, bias_t, lq1, lk1, lq2, lk2, sg, lambda_init):
    seq = k.shape[0]
    t = ATT_TILE
    vec = lambda n: pl.BlockSpec((1, n), lambda h, i: (0, 0))
    return pl.pallas_call(
        functools.partial(_diff_attn_kernel, lambda_init=lambda_init),
        out_shape=jax.ShapeDtypeStruct((seq, DA_WIDTH), BF16),
        grid=(DA_HEADS, seq // t),
        in_specs=[
            pl.BlockSpec((1, 1, DA_V_DIM, t), lambda h, i: (h, i, 0, 0)),
            pl.BlockSpec((seq, DA_V_DIM), lambda h, i: (0, h)),
            pl.BlockSpec((1, seq // t, DA_V_DIM, t), lambda h, i: (h, 0, 0, 0)),
            pl.BlockSpec((1, 2 * t, t), lambda h, i: (h, 0, 0)),
            vec(DA_QK_DIM), vec(DA_QK_DIM), vec(DA_QK_DIM), vec(DA_QK_DIM),
            vec(DA_V_DIM),
        ],
        out_specs=pl.BlockSpec((t, DA_V_DIM), lambda h, i: (i, h)),
        scratch_shapes=[
            pltpu.VMEM((2 * DA_QK_DIM, 2 * t), BF16),
            pltpu.VMEM((DA_V_DIM, 2 * t), F32),
            pltpu.VMEM((1, 2 * t), F32),
            pltpu.VMEM((1, 2 * t), F32),
            pltpu.VMEM((2, t, 2 * t), F32),
        ],
        compiler_params=pltpu.CompilerParams(
            dimension_semantics=("arbitrary", "arbitrary"),
            vmem_limit_bytes=V7X_SCOPED_VMEM_BYTES),
        name="diff_attn",
    )(qt, k, vt, bias_t, lq1, lk1, lq2, lk2, sg)


def _out_ffn_kernel(x_ref, a_ref, b_ref, c_ref, wo_ref, gpm_ref, gpf_ref, gqf_ref,
                    wu_ref, cw_ref, cb_ref, wd_ref, xo_ref, carry, gbuf, vbuf, yacc):
    tm = x_ref.shape[0]
    i = pl.program_id(0)
    n_chunks = D_FF // FF_CHUNK

    @pl.when(i == 0)
    def _():
        carry[...] = jnp.zeros(carry.shape, F32)

    o_a, o_b, o_c = 0, GM_WIDTH, GM_WIDTH + DA_WIDTH
    mix = (jnp.dot(a_ref[...], wo_ref[o_a:o_b, :], preferred_element_type=F32)
           + jnp.dot(b_ref[...], wo_ref[o_b:o_c, :], preferred_element_type=F32)
           + jnp.dot(c_ref[...], wo_ref[o_c:, :], preferred_element_type=F32))
    x1 = x_ref[...] + _rms(mix, gpm_ref[...])
    hb = _rms(x1, gpf_ref[...]).astype(BF16)

    def conv_up(buf, col):
        up = jnp.dot(hb, wu_ref[:, col:col + FF_CHUNK], preferred_element_type=F32)
        buf[0:FFN_HALO, :] = carry[:, col:col + FF_CHUNK]
        buf[FFN_HALO:FFN_HALO + tm, :] = up
        carry[:, col:col + FF_CHUNK] = up[tm - FFN_HALO:tm, :]
        out = cb_ref[:, col:col + FF_CHUNK] + cw_ref[FFN_KERNEL - 1:FFN_KERNEL, col:col + FF_CHUNK] * up
        for kk in range(FFN_KERNEL - 1):
            off = FFN_HALO - (FFN_KERNEL - 1) + kk
            out = out + cw_ref[kk:kk + 1, col:col + FF_CHUNK] * buf[off:off + tm, :]
        return out

    for ch in range(n_chunks):
        col = ch * FF_CHUNK
        gate = conv_up(gbuf, col)
        val = conv_up(vbuf, D_FF + col)
        act = (jax.nn.gelu(gate) * val).astype(BF16)
        contrib = jnp.dot(act, wd_ref[col:col + FF_CHUNK, :], preferred_element_type=F32)
        if ch == 0:
            yacc[...] = contrib
        else:
            yacc[...] += contrib

    xo_ref[...] = x1 + _rms(yacc[...], gqf_ref[...])


def _out_ffn(x, a, b, c, wo_bf, gpm, gpf, gqf, wu_bf, cw, cb, wd_bf):
    seq = x.shape[0]
    tm = ROW_TILE
    rows = lambda n: pl.BlockSpec((tm, n), lambda i: (i, 0))

    def resident(shape):
        return pl.BlockSpec(shape, lambda i: (0,) * len(shape), pipeline_mode=pl.Buffered(1))

    return pl.pallas_call(
        _out_ffn_kernel,
        out_shape=jax.ShapeDtypeStruct((seq, D_MODEL), F32),
        grid=(seq // tm,),
        in_specs=[
            rows(D_MODEL), rows(GM_WIDTH), rows(DA_WIDTH), rows(CV_WIDTH),
            resident((D_MODEL, D_MODEL)),
            resident((1, D_MODEL)), resident((1, D_MODEL)), resident((1, D_MODEL)),
            resident((D_MODEL, 2 * D_FF)),
            resident((V7X_SUBLANES, 2 * D_FF)),
            resident((1, 2 * D_FF)),
            resident((D_FF, D_MODEL)),
        ],
        out_specs=rows(D_MODEL),
        scratch_shapes=[
            pltpu.VMEM((FFN_HALO, 2 * D_FF), F32),
            pltpu.VMEM((FFN_HALO + tm, FF_CHUNK), F32),
            pltpu.VMEM((FFN_HALO + tm, FF_CHUNK), F32),
            pltpu.VMEM((tm, D_MODEL), F32),
        ],
        compiler_params=pltpu.CompilerParams(
            dimension_semantics=("arbitrary",),
            vmem_limit_bytes=V7X_SCOPED_VMEM_BYTES),
        name="out_ffn",
    )(x, a, b, c, wo_bf, gpm, gpf, gqf, wu_bf, cw, cb, wd_bf)


def _bucket_table():
    n = np.arange(MAX_DISTANCE + 1)
    max_exact = N_BUCKETS // 2
    nf = np.maximum(n, 1).astype(np.float32)
    large = max_exact + (np.log(nf / max_exact) / math.log(MAX_DISTANCE / max_exact)
                         * (N_BUCKETS - max_exact)).astype(np.int32)
    large = np.minimum(large, N_BUCKETS - 1)
    return np.where(n < max_exact, n, large).astype(np.int32)


def _bias_tiles(rel_bias):
    t = ATT_TILE
    heads = rel_bias.shape[1]
    table = jnp.transpose((rel_bias - rel_bias[N_BUCKETS - 1][None, :]) * LOG2E)
    near = table[:, _bucket_table()[:MAX_DISTANCE]].astype(F32)
    width = 3 * t
    w = jnp.concatenate([jnp.zeros((heads, t), F32),
                         jnp.full((heads, t), -jnp.inf, F32),
                         near,
                         jnp.zeros((heads, t - MAX_DISTANCE), F32)], axis=1)
    flat = jnp.tile(w, (1, 2 * t))[:, :2 * t * (width - 1)]
    return flat.reshape(heads, 2 * t, width - 1)[:, :, :t]


def _group_mean_matrix():
    g = np.arange(CV_WIDTH) // (CV_WIDTH // CV_GROUPS)
    return jnp.asarray((g[:, None] == g[None, :]) / (CV_WIDTH // CV_GROUPS), dtype=BF16)


def _pad_rows(w, rows):
    return jnp.concatenate([w, jnp.zeros((rows - w.shape[0], w.shape[1]), w.dtype)], axis=0)


def kernel(x, w_in, w_out, gm_ln_g, gm_ln_b, gm_w_s, gm_b_s, da_lq1, da_lk1, da_lq2, da_lk2, da_subln_g, rel_bias, cv_dw_w, cv_dw_b, cv_ln_g, cv_ln_b, ffn_w_up, ffn_conv_w, ffn_conv_b, ffn_w_down, pre_mix_g, post_mix_g, pre_ffn_g, post_ffn_g):
    batch, seq, d_model = x.shape
    depth = w_in.shape[0]
    assert batch == 1 and d_model == D_MODEL and w_in.shape[2] == IN_WIDTH
    assert seq % ROW_TILE == 0 and ROW_TILE == ATT_TILE and ATT_TILE >= MAX_DISTANCE
    assert ROW_TILE % CHUNK == 0 and D_FF % FF_CHUNK == 0 and CV_HALO >= CV_KERNEL - 1

    bias2 = _bias_tiles(rel_bias)
    gmat = _group_mean_matrix()
    row = lambda p: p.reshape(1, -1)

    xs = x[0]
    for l in range(depth):
        lambda_init = 0.8 - 0.6 * math.exp(-0.3 * l)
        bs_full = jnp.repeat(jnp.transpose(gm_b_s[l]), GM_HEAD_DIM, axis=1)
        w_bf = w_in[l].astype(BF16)
        q0, v0 = 2 * GM_WIDTH, 2 * GM_WIDTH + 2 * DA_WIDTH
        a, q, k, v, c = _mix_in(
            xs, row(pre_mix_g[l]), w_bf,
            jnp.transpose(w_bf[:, q0:q0 + DA_WIDTH]), jnp.transpose(w_bf[:, v0:v0 + DA_WIDTH]),
            row(gm_ln_g[l]), row(gm_ln_b[l]), gm_w_s[l], bs_full,
            _pad_rows(cv_dw_w[l], CV_HALO), row(cv_dw_b[l]), row(cv_ln_g[l]), row(cv_ln_b[l]), gmat)
        b = _diff_attn(q, k, v, bias2, row(da_lq1[l]), row(da_lk1[l]), row(da_lq2[l]), row(da_lk2[l]),
                       row(da_subln_g[l]), lambda_init)
        xs = _out_ffn(
            xs, a, b, c, w_out[l].astype(BF16),
            row(post_mix_g[l]), row(pre_ffn_g[l]), row(post_ffn_g[l]),
            ffn_w_up[l].astype(BF16), _pad_rows(ffn_conv_w[l], V7X_SUBLANES), row(ffn_conv_b[l]),
            ffn_w_down[l].astype(BF16))
    return xs[None]
```

```python
import functools
import math

import numpy as np
import jax
import jax.numpy as jnp
from jax import lax
from jax.experimental import pallas as pl
from jax.experimental.pallas import tpu as pltpu

F32 = jnp.float32
BF16 = jnp.bfloat16

D_MODEL = 1024
GM_HEADS = 4
GM_WIDTH = 256
GM_HEAD_DIM = GM_WIDTH // GM_HEADS
CHUNK = 128
DA_HEADS = 4
DA_WIDTH = 512
DA_V_DIM = 128
DA_QK_DIM = 64
CV_GROUPS = 4
CV_WIDTH = 256
CV_KERNEL = 31
D_FF = 2816
FFN_KERNEL = 3
N_BUCKETS = 32
MAX_DISTANCE = 128
EPS = 1e-6
IN_WIDTH = 2 * GM_WIDTH + 3 * DA_WIDTH + 2 * CV_WIDTH

V7X_SUBLANES = 8
V7X_LANES = 128
V7X_MXU_DIM = 256
V7X_SCOPED_VMEM_BYTES = 60000 * 1024

ROW_TILE = 512
ATT_TILE = 512
ATT_GROUP = V7X_MXU_DIM
CV_HALO = 32
FFN_HALO = V7X_SUBLANES
FF_CHUNK = V7X_MXU_DIM
FFN_ROW_SPLIT = 2

LOG2E = math.log2(math.e)


def _rms(x, g):
    return x * lax.rsqrt(jnp.mean(x * x, axis=-1, keepdims=True) + EPS) * g


def _split_dot(x, g_ref):
    hi = x.astype(BF16)
    lo = (x - hi.astype(F32)).astype(BF16)
    g = g_ref[...]
    return (jnp.dot(hi, g, preferred_element_type=F32)
            + jnp.dot(lo, g, preferred_element_type=F32))


def _mix_in_kernel(x_ref, g_ref, w_ref, wqt_ref, wvt_ref, lng_ref, lnb_ref, ws_ref, bs_ref,
                   cw_ref, cb_ref, cg_ref, cbeta_ref, gmat_ref,
                   a_ref, qt_ref, k_ref, vt_ref, c_ref, hbuf):
    tm = x_ref.shape[0]
    i = pl.program_id(0)

    x = x_ref[...]
    hb = _rms(x, g_ref[...]).astype(BF16)

    def proj(c0, n):
        return jnp.dot(hb, w_ref[:, c0:c0 + n], preferred_element_type=F32)

    def proj_t(wt_ref):
        return lax.dot_general(wt_ref[...], hb, (((1,), (1,)), ((), ())), preferred_element_type=F32)

    c0 = 2 * GM_WIDTH

    cv = proj(c0 + 3 * DA_WIDTH, 2 * CV_WIDTH)
    hg = cv[:, :CV_WIDTH] * jax.nn.sigmoid(cv[:, CV_WIDTH:])

    @pl.when(i == 0)
    def _():
        hbuf[0, 0:CV_HALO, :] = jnp.zeros((CV_HALO, CV_WIDTH), F32)

    hbuf[0, CV_HALO:CV_HALO + tm, :] = hg
    n_shift_rows = tm + CV_HALO - V7X_SUBLANES
    for b in range(1, V7X_SUBLANES):
        hbuf[b, 0:n_shift_rows, :] = hbuf[0, b:b + n_shift_rows, :]

    gm = jax.nn.gelu(proj(0, 2 * GM_WIDTH))
    qt = (proj_t(wqt_ref) * (DA_QK_DIM ** -0.5 * LOG2E)).astype(BF16)
    qt_ref[:, 0, :, :] = qt.reshape(DA_HEADS, DA_V_DIM, tm)
    u = gm[:, :GM_WIDTH]
    v = gm[:, GM_WIDTH:]
    mu = jnp.mean(v, axis=-1, keepdims=True)
    d = v - mu
    var = jnp.mean(d * d, axis=-1, keepdims=True)
    vn = d * lax.rsqrt(var + EPS) * lng_ref[...] + lnb_ref[...]

    row = lax.broadcasted_iota(jnp.int32, (CHUNK, CHUNK), 0)
    col = lax.broadcasted_iota(jnp.int32, (CHUNK, CHUNK), 1)
    causal = row >= col
    wcat = jnp.concatenate(
        [jnp.where(causal, ws_ref[h], 0.0).astype(BF16) for h in range(GM_HEADS)], axis=1)
    head_of_lane = lax.broadcasted_iota(jnp.int32, (CHUNK, GM_WIDTH), 1) // GM_HEAD_DIM
    bs = bs_ref[...]
    for c in range(tm // CHUNK):
        r0 = c * CHUNK
        vc = vn[r0:r0 + CHUNK, :]
        vstack = jnp.concatenate(
            [jnp.where(head_of_lane == h, vc, 0.0).astype(BF16) for h in range(GM_HEADS)], axis=0)
        mixed = jnp.dot(wcat, vstack, preferred_element_type=F32) + bs
        a_ref[r0:r0 + CHUNK, :] = (u[r0:r0 + CHUNK, :] * mixed).astype(BF16)

    k_ref[...] = proj(c0 + DA_WIDTH, DA_WIDTH).astype(BF16)
    vt_ref[:, 0, :, :] = proj_t(wvt_ref).astype(BF16).reshape(DA_HEADS, DA_V_DIM, tm)

    acc = jnp.broadcast_to(cb_ref[...], (tm, CV_WIDTH))
    for kk in range(CV_KERNEL):
        off = CV_HALO - (CV_KERNEL - 1) + kk
        b, a0 = off % V7X_SUBLANES, off - off % V7X_SUBLANES
        acc = acc + cw_ref[kk:kk + 1, :] * hbuf[b, a0:a0 + tm, :]
    hbuf[0, 0:CV_HALO, :] = hbuf[0, tm:tm + CV_HALO, :]

    gmu = _split_dot(acc, gmat_ref)
    dd = acc - gmu
    gvar = _split_dot(dd * dd, gmat_ref)
    y = dd * lax.rsqrt(gvar + EPS) * cg_ref[...] + cbeta_ref[...]
    c_ref[...] = (y * jax.nn.sigmoid(y)).astype(BF16)


def _mix_in(x, g, w_bf, wqt_bf, wvt_bf, lng, lnb, ws, bs_full, cw, cb, cg, cbeta, gmat):
    seq = x.shape[0]
    tm = ROW_TILE
    full = lambda shape: pl.BlockSpec(shape, lambda i: (0,) * len(shape))
    rows = lambda n: pl.BlockSpec((tm, n), lambda i: (i, 0))
    slabs = pl.BlockSpec((DA_HEADS, 1, DA_V_DIM, tm), lambda i: (0, i, 0, 0))
    slab_shape = jax.ShapeDtypeStruct((DA_HEADS, seq // tm, DA_V_DIM, tm), BF16)
    out_shapes = (
        jax.ShapeDtypeStruct((seq, GM_WIDTH), BF16),
        slab_shape,
        jax.ShapeDtypeStruct((seq, DA_WIDTH), BF16),
        slab_shape,
        jax.ShapeDtypeStruct((seq, CV_WIDTH), BF16),
    )
    return pl.pallas_call(
        _mix_in_kernel,
        out_shape=out_shapes,
        grid=(seq // tm,),
        in_specs=[
            rows(D_MODEL),
            full((1, D_MODEL)),
            full((D_MODEL, IN_WIDTH)),
            full((DA_WIDTH, D_MODEL)), full((DA_WIDTH, D_MODEL)),
            full((1, GM_WIDTH)), full((1, GM_WIDTH)),
            full((GM_HEADS, CHUNK, CHUNK)),
            full((CHUNK, GM_WIDTH)),
            full((CV_HALO, CV_WIDTH)),
            full((1, CV_WIDTH)), full((1, CV_WIDTH)), full((1, CV_WIDTH)),
            full((CV_WIDTH, CV_WIDTH)),
        ],
        out_specs=(rows(GM_WIDTH), slabs, rows(DA_WIDTH), slabs, rows(CV_WIDTH)),
        scratch_shapes=[pltpu.VMEM((V7X_SUBLANES, CV_HALO + tm, CV_WIDTH), F32)],
        compiler_params=pltpu.CompilerParams(
            dimension_semantics=("arbitrary",),
            vmem_limit_bytes=V7X_SCOPED_VMEM_BYTES),
        name="mix_in",
    )(x, g, w_bf, wqt_bf, wvt_bf, lng, lnb, ws, bs_full, cw, cb, cg, cbeta, gmat)


def _fill_bias(rb_ref, bias_s, head, t):
    sub = V7X_LANES
    table = _bucket_table()
    first_rel = [int(np.argmax(table == b)) for b in range(N_BUCKETS)]
    last = rb_ref[N_BUCKETS - 1, head]
    vals = [(rb_ref[b, head] - last) * LOG2E for b in range(N_BUCKETS)]
    d0 = (lax.broadcasted_iota(jnp.int32, (sub, sub), 1)
          - lax.broadcasted_iota(jnp.int32, (sub, sub), 0))

    def band(base):
        rel = d0 + base
        v = jnp.full((sub, sub), vals[0], F32)
        for b in range(1, N_BUCKETS):
            v = jnp.where(rel >= first_rel[b], vals[b], v)
        return jnp.where(rel < 0, -jnp.inf, v)

    bands = {0: band(0), sub: band(sub)}
    for cb in range(2 * t // sub):
        for rb in range(t // sub):
            base = sub * (rb - cb) + t
            if base in bands:
                tile = bands[base]
            elif base > sub:
                tile = jnp.zeros((sub, sub), F32)
            else:
                tile = jnp.full((sub, sub), -jnp.inf, F32)
            bias_s[cb * sub:(cb + 1) * sub, rb * sub:(rb + 1) * sub] = tile


def _diff_attn_kernel(rb_ref, qt_ref, k_ref, vt_ref, lq1_ref, lk1_ref, lq2_ref, lk2_ref, sg_ref,
                      o_ref, qst, acc, m_s, l_s, s_buf, mx_buf, bias_s, *, lambda_init):
    t = o_ref.shape[0]
    i = pl.program_id(1)

    @pl.when(i == 0)
    def _():
        _fill_bias(rb_ref, bias_s, pl.program_id(0), t)

    qt = qt_ref[0, 0]
    chan = lax.broadcasted_iota(jnp.int32, qt.shape, 0)
    zero = jnp.zeros_like(qt)
    qst[:, 0:t] = jnp.where(chan < DA_QK_DIM, qt, zero)
    qst[:, t:2 * t] = jnp.where(chan >= DA_QK_DIM, qt, zero)
    m_s[...] = jnp.full(m_s.shape, -jnp.inf, F32)
    l_s[...] = jnp.zeros(l_s.shape, F32)
    acc[...] = jnp.zeros(acc.shape, F32)

    n_groups = 2 * t // ATT_GROUP
    cols = [slice(g * ATT_GROUP, (g + 1) * ATT_GROUP) for g in range(n_groups)]

    def scores(j, slot):
        kb = k_ref[pl.ds(pl.multiple_of(j * t, t), t), :]
        for g in range(n_groups):
            s = jnp.dot(kb, qst[:, cols[g]], preferred_element_type=F32)
            s_buf[slot, :, cols[g]] = s
            mx_buf[slot, :, cols[g]] = jnp.max(s, axis=0, keepdims=True)

    def softmax_pv(j, slot, bias_row0):
        for g in range(n_groups):
            s = s_buf[slot, :, cols[g]]
            if bias_row0 is None:
                s_max = mx_buf[slot, :, cols[g]]
            else:
                b0 = (g * ATT_GROUP) % t
                s = s + bias_s[bias_row0:bias_row0 + t, b0:b0 + ATT_GROUP]
                s_max = jnp.max(s, axis=0, keepdims=True)
            m_prev = m_s[:, cols[g]]
            m_new = jnp.maximum(m_prev, s_max)
            alpha = jnp.exp2(m_prev - m_new)
            p = jnp.exp2(s - m_new)
            l_s[:, cols[g]] = alpha * l_s[:, cols[g]] + jnp.sum(p, axis=0, keepdims=True)
            acc[:, cols[g]] = (alpha * acc[:, cols[g]]
                               + jnp.dot(vt_ref[0, j], p.astype(BF16), preferred_element_type=F32))
            m_s[:, cols[g]] = m_new

    n_far = i - 1
    odd = jnp.maximum(n_far, 0) % 2
    scores(0, 0)

    @pl.when(odd == 1)
    def _():
        softmax_pv(0, 0, None)
        scores(1, 0)

    def far_pair(pp, carry):
        j = odd + 2 * pp
        scores(j + 1, 1)
        softmax_pv(j, 0, None)
        scores(j + 2, 0)
        softmax_pv(j + 1, 1, None)
        return carry

    lax.fori_loop(0, jnp.maximum(n_far, 0) // 2, far_pair, 0)

    @pl.when(i > 0)
    def _():
        scores(i, 1)
        softmax_pv(i - 1, 0, 0)
        softmax_pv(i, 1, t)

    @pl.when(i == 0)
    def _():
        softmax_pv(0, 0, t)

    o1 = acc[:, 0:t] / l_s[:, 0:t]
    o2 = acc[:, t:2 * t] / l_s[:, t:2 * t]
    lam = (jnp.exp(jnp.sum(lq1_ref[...] * lk1_ref[...], axis=-1, keepdims=True))
           - jnp.exp(jnp.sum(lq2_ref[...] * lk2_ref[...], axis=-1, keepdims=True))
           + lambda_init)
    ot = o1 - lam * o2
    ot = ot * lax.rsqrt(jnp.mean(ot * ot, axis=0, keepdims=True) + EPS)
    o_ref[...] = (jnp.transpose(ot) * sg_ref[...] * (1.0 - lambda_init)).astype(BF16)


def _diff_attn(rel_bias, qt, k, vt, lq1, lk1, lq2, lk2, sg, lambda_init):
    seq = k.shape[0]
    t = ATT_TILE
    vec = lambda n: pl.BlockSpec((1, n), lambda h, i: (0, 0))
    return pl.pallas_call(
        functools.partial(_diff_attn_kernel, lambda_init=lambda_init),
        out_shape=jax.ShapeDtypeStruct((seq, DA_WIDTH), BF16),
        grid=(DA_HEADS, seq // t),
        in_specs=[
            pl.BlockSpec(memory_space=pltpu.SMEM),
            pl.BlockSpec((1, 1, DA_V_DIM, t), lambda h, i: (h, i, 0, 0)),
            pl.BlockSpec((seq, DA_V_DIM), lambda h, i: (0, h)),
            pl.BlockSpec((1, seq // t, DA_V_DIM, t), lambda h, i: (h, 0, 0, 0)),
            vec(DA_QK_DIM), vec(DA_QK_DIM), vec(DA_QK_DIM), vec(DA_QK_DIM),
            vec(DA_V_DIM),
        ],
        out_specs=pl.BlockSpec((t, DA_V_DIM), lambda h, i: (i, h)),
        scratch_shapes=[
            pltpu.VMEM((2 * DA_QK_DIM, 2 * t), BF16),
            pltpu.VMEM((DA_V_DIM, 2 * t), F32),
            pltpu.VMEM((1, 2 * t), F32),
            pltpu.VMEM((1, 2 * t), F32),
            pltpu.VMEM((2, t, 2 * t), F32),
            pltpu.VMEM((2, 1, 2 * t), F32),
            pltpu.VMEM((2 * t, t), F32),
        ],
        compiler_params=pltpu.CompilerParams(
            dimension_semantics=("arbitrary", "arbitrary"),
            vmem_limit_bytes=V7X_SCOPED_VMEM_BYTES),
        name="diff_attn",
    )(rel_bias, qt, k, vt, lq1, lk1, lq2, lk2, sg)


def _out_ffn_kernel(x_ref, a_ref, b_ref, c_ref, wo_ref, gpm_ref, gpf_ref, gqf_ref,
                    wu_ref, cw_ref, cb_ref, wd_ref, xo_ref, carry, ubuf0, ubuf1, yacc):
    tm = x_ref.shape[0]
    i = pl.program_id(0)
    n_chunks = D_FF // FF_CHUNK

    @pl.when(i == 0)
    def _():
        carry[...] = jnp.zeros(carry.shape, F32)

    o_a, o_b, o_c = 0, GM_WIDTH, GM_WIDTH + DA_WIDTH
    mix = (jnp.dot(a_ref[...], wo_ref[o_a:o_b, :], preferred_element_type=F32)
           + jnp.dot(b_ref[...], wo_ref[o_b:o_c, :], preferred_element_type=F32)
           + jnp.dot(c_ref[...], wo_ref[o_c:, :], preferred_element_type=F32))
    x1 = x_ref[...] + _rms(mix, gpm_ref[...])
    hb = _rms(x1, gpf_ref[...]).astype(BF16)

    rb = tm // FFN_ROW_SPLIT

    def up_proj(ch, r):
        bufs = (ubuf0, ubuf1)[ch % 2]
        r0 = r * rb
        for half in range(2):
            col = half * D_FF + ch * FF_CHUNK
            up = jnp.dot(hb[r0:r0 + rb, :], wu_ref[:, col:col + FF_CHUNK], preferred_element_type=F32)
            if r == 0:
                bufs[half, 0:FFN_HALO, :] = carry[:, col:col + FF_CHUNK]
            bufs[half, FFN_HALO + r0:FFN_HALO + r0 + rb, :] = up
            if r == FFN_ROW_SPLIT - 1:
                carry[:, col:col + FF_CHUNK] = up[rb - FFN_HALO:rb, :]

    def conv(ch, half, r):
        bufs = (ubuf0, ubuf1)[ch % 2]
        col = half * D_FF + ch * FF_CHUNK
        out = cb_ref[:, col:col + FF_CHUNK]
        for kk in range(FFN_KERNEL):
            off = FFN_HALO - (FFN_KERNEL - 1) + kk + r * rb
            out = out + cw_ref[kk:kk + 1, col:col + FF_CHUNK] * bufs[half, off:off + rb, :]
        return out

    for r in range(FFN_ROW_SPLIT):
        up_proj(0, r)
    for ch in range(n_chunks):
        col = ch * FF_CHUNK
        for r in range(FFN_ROW_SPLIT):
            if ch + 1 < n_chunks:
                up_proj(ch + 1, r)
            act = (jax.nn.gelu(conv(ch, 0, r)) * conv(ch, 1, r)).astype(BF16)
            contrib = jnp.dot(act, wd_ref[col:col + FF_CHUNK, :], preferred_element_type=F32)
            rows = slice(r * rb, (r + 1) * rb)
            if ch == 0:
                yacc[rows, :] = contrib
            else:
                yacc[rows, :] += contrib

    xo_ref[...] = x1 + _rms(yacc[...], gqf_ref[...])


def _out_ffn(x, a, b, c, wo_bf, gpm, gpf, gqf, wu_bf, cw, cb, wd_bf):
    seq = x.shape[0]
    tm = ROW_TILE
    rows = lambda n: pl.BlockSpec((tm, n), lambda i: (i, 0))

    def resident(shape):
        return pl.BlockSpec(shape, lambda i: (0,) * len(shape), pipeline_mode=pl.Buffered(1))

    return pl.pallas_call(
        _out_ffn_kernel,
        out_shape=jax.ShapeDtypeStruct((seq, D_MODEL), F32),
        grid=(seq // tm,),
        in_specs=[
            rows(D_MODEL), rows(GM_WIDTH), rows(DA_WIDTH), rows(CV_WIDTH),
            resident((D_MODEL, D_MODEL)),
            resident((1, D_MODEL)), resident((1, D_MODEL)), resident((1, D_MODEL)),
            resident((D_MODEL, 2 * D_FF)),
            resident((V7X_SUBLANES, 2 * D_FF)),
            resident((1, 2 * D_FF)),
            resident((D_FF, D_MODEL)),
        ],
        out_specs=rows(D_MODEL),
        scratch_shapes=[
            pltpu.VMEM((FFN_HALO, 2 * D_FF), F32),
            pltpu.VMEM((2, FFN_HALO + tm, FF_CHUNK), F32),
            pltpu.VMEM((2, FFN_HALO + tm, FF_CHUNK), F32),
            pltpu.VMEM((tm, D_MODEL), F32),
        ],
        compiler_params=pltpu.CompilerParams(
            dimension_semantics=("arbitrary",),
            vmem_limit_bytes=V7X_SCOPED_VMEM_BYTES),
        name="out_ffn",
    )(x, a, b, c, wo_bf, gpm, gpf, gqf, wu_bf, cw, cb, wd_bf)


def _bucket_table():
    n = np.arange(MAX_DISTANCE + 1)
    max_exact = N_BUCKETS // 2
    nf = np.maximum(n, 1).astype(np.float32)
    large = max_exact + (np.log(nf / max_exact) / math.log(MAX_DISTANCE / max_exact)
                         * (N_BUCKETS - max_exact)).astype(np.int32)
    large = np.minimum(large, N_BUCKETS - 1)
    return np.where(n < max_exact, n, large).astype(np.int32)


def _group_mean_matrix():
    g = np.arange(CV_WIDTH) // (CV_WIDTH // CV_GROUPS)
    return jnp.asarray((g[:, None] == g[None, :]) / (CV_WIDTH // CV_GROUPS), dtype=BF16)


def _pad_rows(w, rows):
    return jnp.concatenate([w, jnp.zeros((rows - w.shape[0], w.shape[1]), w.dtype)], axis=0)


def kernel(x, w_in, w_out, gm_ln_g, gm_ln_b, gm_w_s, gm_b_s, da_lq1, da_lk1, da_lq2, da_lk2, da_subln_g, rel_bias, cv_dw_w, cv_dw_b, cv_ln_g, cv_ln_b, ffn_w_up, ffn_conv_w, ffn_conv_b, ffn_w_down, pre_mix_g, post_mix_g, pre_ffn_g, post_ffn_g):
    batch, seq, d_model = x.shape
    depth = w_in.shape[0]
    assert batch == 1 and d_model == D_MODEL and w_in.shape[2] == IN_WIDTH
    assert seq % ROW_TILE == 0 and ROW_TILE == ATT_TILE and ATT_TILE >= MAX_DISTANCE
    assert ROW_TILE % CHUNK == 0 and D_FF % FF_CHUNK == 0 and CV_HALO >= CV_KERNEL - 1

    gmat = _group_mean_matrix()
    row = lambda p: p.reshape(1, -1)

    xs = x[0]
    for l in range(depth):
        lambda_init = 0.8 - 0.6 * math.exp(-0.3 * l)
        bs_full = jnp.repeat(jnp.transpose(gm_b_s[l]), GM_HEAD_DIM, axis=1)
        w_bf = w_in[l].astype(BF16)
        q0, v0 = 2 * GM_WIDTH, 2 * GM_WIDTH + 2 * DA_WIDTH
        a, q, k, v, c = _mix_in(
            xs, row(pre_mix_g[l]), w_bf,
            jnp.transpose(w_bf[:, q0:q0 + DA_WIDTH]), jnp.transpose(w_bf[:, v0:v0 + DA_WIDTH]),
            row(gm_ln_g[l]), row(gm_ln_b[l]), gm_w_s[l], bs_full,
            _pad_rows(cv_dw_w[l], CV_HALO), row(cv_dw_b[l]), row(cv_ln_g[l]), row(cv_ln_b[l]), gmat)
        b = _diff_attn(rel_bias, q, k, v, row(da_lq1[l]), row(da_lk1[l]), row(da_lq2[l]), row(da_lk2[l]),
                       row(da_subln_g[l]), lambda_init)
        xs = _out_ffn(
            xs, a, b, c, w_out[l].astype(BF16),
            row(post_mix_g[l]), row(pre_ffn_g[l]), row(post_ffn_g[l]),
            ffn_w_up[l].astype(BF16), _pad_rows(ffn_conv_w[l], V7X_SUBLANES), row(ffn_conv_b[l]),
            ffn_w_down[l].astype(BF16))
    return xs[None]
```

```python
import functools
import math

import numpy as np
import jax
import jax.numpy as jnp
from jax import lax
from jax.experimental import pallas as pl
from jax.experimental.pallas import tpu as pltpu

F32 = jnp.float32
BF16 = jnp.bfloat16

D_MODEL = 1024
GM_HEADS = 4
GM_WIDTH = 256
GM_HEAD_DIM = GM_WIDTH // GM_HEADS
CHUNK = 128
DA_HEADS = 4
DA_WIDTH = 512
DA_V_DIM = 128
DA_QK_DIM = 64
CV_GROUPS = 4
CV_WIDTH = 256
CV_KERNEL = 31
D_FF = 2816
FFN_KERNEL = 3
N_BUCKETS = 32
MAX_DISTANCE = 128
EPS = 1e-6
IN_WIDTH = 2 * GM_WIDTH + 3 * DA_WIDTH + 2 * CV_WIDTH

V7X_SUBLANES = 8
V7X_LANES = 128
V7X_MXU_DIM = 256
V7X_SCOPED_VMEM_BYTES = 60000 * 1024

ROW_TILE = 512
ATT_TILE = 512
ATT_Q_SLABS = 2
ATT_GROUP = V7X_MXU_DIM
CV_HALO = 32
FFN_HALO = V7X_SUBLANES
FF_CHUNK = V7X_MXU_DIM
FFN_ROW_SPLIT = 2

LOG2E = math.log2(math.e)


_GELU_K1 = -2.0 * math.sqrt(2.0 / math.pi) * LOG2E
_GELU_K3 = _GELU_K1 * 0.044715


def _gelu(x):
    return x / (1.0 + jnp.exp2(x * (_GELU_K1 + _GELU_K3 * (x * x))))


def _rms(x, g):
    return x * lax.rsqrt(jnp.mean(x * x, axis=-1, keepdims=True) + EPS) * g


def _split_dot(x, g_ref):
    hi = x.astype(BF16)
    lo = (x - hi.astype(F32)).astype(BF16)
    g = g_ref[...]
    return (jnp.dot(hi, g, preferred_element_type=F32)
            + jnp.dot(lo, g, preferred_element_type=F32))


def _mix_in_kernel(x_ref, g_ref, w_ref, wqt_ref, wvt_ref, lng_ref, lnb_ref, ws_ref, bs_ref,
                   cw_ref, cb_ref, cg_ref, cbeta_ref, gmat_ref,
                   a_ref, qt_ref, k_ref, vt_ref, c_ref, hbuf):
    tm = x_ref.shape[0]
    i = pl.program_id(0)

    x = x_ref[...]
    hb = _rms(x, g_ref[...]).astype(BF16)

    def proj(c0, n):
        return jnp.dot(hb, w_ref[:, c0:c0 + n], preferred_element_type=F32)

    def proj_t(wt_ref):
        return lax.dot_general(wt_ref[...], hb, (((1,), (1,)), ((), ())), preferred_element_type=F32)

    c0 = 2 * GM_WIDTH

    cv = proj(c0 + 3 * DA_WIDTH, 2 * CV_WIDTH)
    hg = cv[:, :CV_WIDTH] * jax.nn.sigmoid(cv[:, CV_WIDTH:])

    @pl.when(i == 0)
    def _():
        hbuf[0, 0:CV_HALO, :] = jnp.zeros((CV_HALO, CV_WIDTH), F32)

    hbuf[0, CV_HALO:CV_HALO + tm, :] = hg
    n_shift_rows = tm + CV_HALO - V7X_SUBLANES
    for b in range(1, V7X_SUBLANES):
        hbuf[b, 0:n_shift_rows, :] = hbuf[0, b:b + n_shift_rows, :]

    gm = _gelu(proj(0, 2 * GM_WIDTH))
    qt = (proj_t(wqt_ref) * (DA_QK_DIM ** -0.5 * LOG2E)).astype(BF16)
    qt_ref[:, 0, :, :] = qt.reshape(DA_HEADS, DA_V_DIM, tm)
    u = gm[:, :GM_WIDTH]
    v = gm[:, GM_WIDTH:]
    mu = jnp.mean(v, axis=-1, keepdims=True)
    d = v - mu
    var = jnp.mean(d * d, axis=-1, keepdims=True)
    vn = d * lax.rsqrt(var + EPS) * lng_ref[...] + lnb_ref[...]

    row = lax.broadcasted_iota(jnp.int32, (CHUNK, CHUNK), 0)
    col = lax.broadcasted_iota(jnp.int32, (CHUNK, CHUNK), 1)
    causal = row >= col
    wcat = jnp.concatenate(
        [jnp.where(causal, ws_ref[h], 0.0).astype(BF16) for h in range(GM_HEADS)], axis=1)
    head_of_lane = lax.broadcasted_iota(jnp.int32, (CHUNK, GM_WIDTH), 1) // GM_HEAD_DIM
    bs = bs_ref[...]
    for c in range(tm // CHUNK):
        r0 = c * CHUNK
        vc = vn[r0:r0 + CHUNK, :]
        vstack = jnp.concatenate(
            [jnp.where(head_of_lane == h, vc, 0.0).astype(BF16) for h in range(GM_HEADS)], axis=0)
        mixed = jnp.dot(wcat, vstack, preferred_element_type=F32) + bs
        a_ref[r0:r0 + CHUNK, :] = (u[r0:r0 + CHUNK, :] * mixed).astype(BF16)

    k_ref[...] = proj(c0 + DA_WIDTH, DA_WIDTH).astype(BF16)
    vt_ref[:, 0, :, :] = proj_t(wvt_ref).astype(BF16).reshape(DA_HEADS, DA_V_DIM, tm)

    acc = jnp.broadcast_to(cb_ref[...], (tm, CV_WIDTH))
    for kk in range(CV_KERNEL):
        off = CV_HALO - (CV_KERNEL - 1) + kk
        b, a0 = off % V7X_SUBLANES, off - off % V7X_SUBLANES
        acc = acc + cw_ref[kk:kk + 1, :] * hbuf[b, a0:a0 + tm, :]
    hbuf[0, 0:CV_HALO, :] = hbuf[0, tm:tm + CV_HALO, :]

    gmu = _split_dot(acc, gmat_ref)
    dd = acc - gmu
    gvar = _split_dot(dd * dd, gmat_ref)
    y = dd * lax.rsqrt(gvar + EPS) * cg_ref[...] + cbeta_ref[...]
    c_ref[...] = (y * jax.nn.sigmoid(y)).astype(BF16)


def _mix_in(x, g, w_bf, wqt_bf, wvt_bf, lng, lnb, ws, bs_full, cw, cb, cg, cbeta, gmat):
    seq = x.shape[0]
    tm = ROW_TILE
    full = lambda shape: pl.BlockSpec(shape, lambda i: (0,) * len(shape))
    rows = lambda n: pl.BlockSpec((tm, n), lambda i: (i, 0))
    slabs = pl.BlockSpec((DA_HEADS, 1, DA_V_DIM, tm), lambda i: (0, i, 0, 0))
    slab_shape = jax.ShapeDtypeStruct((DA_HEADS, seq // tm, DA_V_DIM, tm), BF16)
    out_shapes = (
        jax.ShapeDtypeStruct((seq, GM_WIDTH), BF16),
        slab_shape,
        jax.ShapeDtypeStruct((seq, DA_WIDTH), BF16),
        slab_shape,
        jax.ShapeDtypeStruct((seq, CV_WIDTH), BF16),
    )
    return pl.pallas_call(
        _mix_in_kernel,
        out_shape=out_shapes,
        grid=(seq // tm,),
        in_specs=[
            rows(D_MODEL),
            full((1, D_MODEL)),
            full((D_MODEL, IN_WIDTH)),
            full((DA_WIDTH, D_MODEL)), full((DA_WIDTH, D_MODEL)),
            full((1, GM_WIDTH)), full((1, GM_WIDTH)),
            full((GM_HEADS, CHUNK, CHUNK)),
            full((CHUNK, GM_WIDTH)),
            full((CV_HALO, CV_WIDTH)),
            full((1, CV_WIDTH)), full((1, CV_WIDTH)), full((1, CV_WIDTH)),
            full((CV_WIDTH, CV_WIDTH)),
        ],
        out_specs=(rows(GM_WIDTH), slabs, rows(DA_WIDTH), slabs, rows(CV_WIDTH)),
        scratch_shapes=[pltpu.VMEM((V7X_SUBLANES, CV_HALO + tm, CV_WIDTH), F32)],
        compiler_params=pltpu.CompilerParams(
            dimension_semantics=("arbitrary",),
            vmem_limit_bytes=V7X_SCOPED_VMEM_BYTES),
        name="mix_in",
    )(x, g, w_bf, wqt_bf, wvt_bf, lng, lnb, ws, bs_full, cw, cb, cg, cbeta, gmat)


def _fill_bias(rb_ref, bias_s, head, t):
    sub = V7X_LANES
    table = _bucket_table()
    first_rel = [int(np.argmax(table == b)) for b in range(N_BUCKETS)]
    last = rb_ref[N_BUCKETS - 1, head]
    vals = [(rb_ref[b, head] - last) * LOG2E for b in range(N_BUCKETS)]
    d0 = (lax.broadcasted_iota(jnp.int32, (sub, sub), 1)
          - lax.broadcasted_iota(jnp.int32, (sub, sub), 0))

    def band(base):
        rel = d0 + base
        v = jnp.full((sub, sub), vals[0], F32)
        for b in range(1, N_BUCKETS):
            v = jnp.where(rel >= first_rel[b], vals[b], v)
        return jnp.where(rel < 0, -jnp.inf, v)

    bands = {0: band(0), sub: band(sub)}
    for cb in range(bias_s.shape[0] // sub):
        for rb in range(bias_s.shape[1] // sub):
            base = sub * (rb - cb) + t
            if base in bands:
                tile = bands[base]
            elif base > sub:
                tile = jnp.zeros((sub, sub), F32)
            else:
                tile = jnp.full((sub, sub), -jnp.inf, F32)
            bias_s[cb * sub:(cb + 1) * sub, rb * sub:(rb + 1) * sub] = tile


def _diff_attn_kernel(rb_ref, qt_ref, k_ref, vt_ref, lq1_ref, lk1_ref, lq2_ref, lk2_ref, sg_ref,
                      o_ref, qst, acc, m_s, l_s, s_buf, mx_buf, bias_s, *, lambda_init):
    t = ATT_TILE
    tq = o_ref.shape[0]
    i = pl.program_id(1)

    @pl.when(i == 0)
    def _():
        _fill_bias(rb_ref, bias_s, pl.program_id(0), t)

    for slab in range(ATT_Q_SLABS):
        qt = qt_ref[0, slab]
        chan = lax.broadcasted_iota(jnp.int32, qt.shape, 0)
        zero = jnp.zeros_like(qt)
        qst[:, slab * t:(slab + 1) * t] = jnp.where(chan < DA_QK_DIM, qt, zero)
        qst[:, tq + slab * t:tq + (slab + 1) * t] = jnp.where(chan >= DA_QK_DIM, qt, zero)
    m_s[...] = jnp.full(m_s.shape, -jnp.inf, F32)
    l_s[...] = jnp.zeros(l_s.shape, F32)
    acc[...] = jnp.zeros(acc.shape, F32)

    n_groups = 2 * tq // ATT_GROUP
    cols = [slice(g * ATT_GROUP, (g + 1) * ATT_GROUP) for g in range(n_groups)]

    def group_kind(g, d):
        r0 = (g * ATT_GROUP) % tq
        rel_min = r0 + t * (1 - d) - (t - 1)
        rel_max = r0 + ATT_GROUP - 1 + t * (1 - d)
        if rel_max < 0:
            return "masked"
        return "plain" if rel_min >= MAX_DISTANCE else "biased"

    def scores(j, slot, d=None):
        kb = k_ref[pl.ds(pl.multiple_of(j * t, t), t), :]
        for g in range(n_groups):
            if d is not None and group_kind(g, d) == "masked":
                continue
            s = jnp.dot(kb, qst[:, cols[g]], preferred_element_type=F32)
            s_buf[slot, g] = s
            mx_buf[slot, :, cols[g]] = jnp.max(s, axis=0, keepdims=True)

    def softmax_pv(j, slot, d=None):
        for g in range(n_groups):
            kind = "plain" if d is None else group_kind(g, d)
            if kind == "masked":
                continue
            s = s_buf[slot, g]
            if kind == "plain":
                s_max = mx_buf[slot, :, cols[g]]
            else:
                r0 = (g * ATT_GROUP) % tq
                s = s + bias_s[d * t:(d + 1) * t, r0:r0 + ATT_GROUP]
                s_max = jnp.max(s, axis=0, keepdims=True)
            m_prev = m_s[:, cols[g]]
            m_new = jnp.maximum(m_prev, s_max)
            alpha = jnp.exp2(m_prev - m_new)
            p = jnp.exp2(s - m_new)
            l_s[:, cols[g]] = alpha * l_s[:, cols[g]] + jnp.sum(p, axis=0, keepdims=True)
            acc[:, cols[g]] = (alpha * acc[:, cols[g]]
                               + jnp.dot(vt_ref[0, j], p.astype(BF16), preferred_element_type=F32))
            m_s[:, cols[g]] = m_new

    first = ATT_Q_SLABS * i
    scores(0, 0)

    def pair(j, d_second):
        scores(j + 1, 1)
        softmax_pv(j, 0)
        scores(j + 2, 0)
        softmax_pv(j + 1, 1, d_second)

    def far_pair(pp, carry):
        pair(2 * pp, None)
        return carry

    lax.fori_loop(0, i - 1, far_pair, 0)

    @pl.when(i > 0)
    def _():
        pair(first - 2, 0)

    scores(first + 1, 1, 2)
    softmax_pv(first, 0, 1)
    softmax_pv(first + 1, 1, 2)

    o1 = acc[:, 0:tq] / l_s[:, 0:tq]
    o2 = acc[:, tq:2 * tq] / l_s[:, tq:2 * tq]
    lam = (jnp.exp(jnp.sum(lq1_ref[...] * lk1_ref[...], axis=-1, keepdims=True))
           - jnp.exp(jnp.sum(lq2_ref[...] * lk2_ref[...], axis=-1, keepdims=True))
           + lambda_init)
    ot = o1 - lam * o2
    ot = ot * lax.rsqrt(jnp.mean(ot * ot, axis=0, keepdims=True) + EPS)
    o_ref[...] = (jnp.transpose(ot) * sg_ref[...] * (1.0 - lambda_init)).astype(BF16)


def _diff_attn(rel_bias, qt, k, vt, lq1, lk1, lq2, lk2, sg, lambda_init):
    seq = k.shape[0]
    t = ATT_TILE
    tq = ATT_Q_SLABS * t
    vec = lambda n: pl.BlockSpec((1, n), lambda h, i: (0, 0))
    return pl.pallas_call(
        functools.partial(_diff_attn_kernel, lambda_init=lambda_init),
        out_shape=jax.ShapeDtypeStruct((seq, DA_WIDTH), BF16),
        grid=(DA_HEADS, seq // tq),
        in_specs=[
            pl.BlockSpec(memory_space=pltpu.SMEM),
            pl.BlockSpec((1, ATT_Q_SLABS, DA_V_DIM, t), lambda h, i: (h, i, 0, 0)),
            pl.BlockSpec((seq, DA_V_DIM), lambda h, i: (0, h)),
            pl.BlockSpec((1, seq // t, DA_V_DIM, t), lambda h, i: (h, 0, 0, 0)),
            vec(DA_QK_DIM), vec(DA_QK_DIM), vec(DA_QK_DIM), vec(DA_QK_DIM),
            vec(DA_V_DIM),
        ],
        out_specs=pl.BlockSpec((tq, DA_V_DIM), lambda h, i: (i, h)),
        scratch_shapes=[
            pltpu.VMEM((2 * DA_QK_DIM, 2 * tq), BF16),
            pltpu.VMEM((DA_V_DIM, 2 * tq), F32),
            pltpu.VMEM((1, 2 * tq), F32),
            pltpu.VMEM((1, 2 * tq), F32),
            pltpu.VMEM((2, 2 * tq // ATT_GROUP, t, ATT_GROUP), F32),
            pltpu.VMEM((2, 1, 2 * tq), F32),
            pltpu.VMEM(((ATT_Q_SLABS + 1) * t, tq), F32),
        ],
        compiler_params=pltpu.CompilerParams(
            dimension_semantics=("arbitrary", "arbitrary"),
            vmem_limit_bytes=V7X_SCOPED_VMEM_BYTES),
        name="diff_attn",
    )(rel_bias, qt, k, vt, lq1, lk1, lq2, lk2, sg)


def _out_ffn_kernel(x_ref, a_ref, b_ref, c_ref, wo_ref, gpm_ref, gpf_ref, gqf_ref,
                    wu_ref, cw_ref, cb_ref, wd_ref, xo_ref, carry, ubuf0, ubuf1, yacc):
    tm = x_ref.shape[0]
    i = pl.program_id(0)
    n_chunks = D_FF // FF_CHUNK

    @pl.when(i == 0)
    def _():
        carry[...] = jnp.zeros(carry.shape, F32)

    o_a, o_b, o_c = 0, GM_WIDTH, GM_WIDTH + DA_WIDTH
    mix = (jnp.dot(a_ref[...], wo_ref[o_a:o_b, :], preferred_element_type=F32)
           + jnp.dot(b_ref[...], wo_ref[o_b:o_c, :], preferred_element_type=F32)
           + jnp.dot(c_ref[...], wo_ref[o_c:, :], preferred_element_type=F32))
    x1 = x_ref[...] + _rms(mix, gpm_ref[...])
    hb = _rms(x1, gpf_ref[...]).astype(BF16)

    rb = tm // FFN_ROW_SPLIT

    def up_proj(ch, r):
        bufs = (ubuf0, ubuf1)[ch % 2]
        r0 = r * rb
        for half in range(2):
            col = half * D_FF + ch * FF_CHUNK
            up = jnp.dot(hb[r0:r0 + rb, :], wu_ref[:, col:col + FF_CHUNK], preferred_element_type=F32)
            if r == 0:
                bufs[half, 0:FFN_HALO, :] = carry[:, col:col + FF_CHUNK]
            bufs[half, FFN_HALO + r0:FFN_HALO + r0 + rb, :] = up
            if r == FFN_ROW_SPLIT - 1:
                carry[:, col:col + FF_CHUNK] = up[rb - FFN_HALO:rb, :]

    def conv(ch, half, r):
        bufs = (ubuf0, ubuf1)[ch % 2]
        col = half * D_FF + ch * FF_CHUNK
        out = cb_ref[:, col:col + FF_CHUNK]
        for kk in range(FFN_KERNEL):
            off = FFN_HALO - (FFN_KERNEL - 1) + kk + r * rb
            out = out + cw_ref[kk:kk + 1, col:col + FF_CHUNK] * bufs[half, off:off + rb, :]
        return out

    for r in range(FFN_ROW_SPLIT):
        up_proj(0, r)
    for ch in range(n_chunks):
        col = ch * FF_CHUNK
        for r in range(FFN_ROW_SPLIT):
            if ch + 1 < n_chunks:
                up_proj(ch + 1, r)
            act = (_gelu(conv(ch, 0, r)) * conv(ch, 1, r)).astype(BF16)
            contrib = jnp.dot(act, wd_ref[col:col + FF_CHUNK, :], preferred_element_type=F32)
            rows = slice(r * rb, (r + 1) * rb)
            if ch == 0:
                yacc[rows, :] = contrib
            else:
                yacc[rows, :] += contrib

    xo_ref[...] = x1 + _rms(yacc[...], gqf_ref[...])


def _out_ffn(x, a, b, c, wo_bf, gpm, gpf, gqf, wu_bf, cw, cb, wd_bf):
    seq = x.shape[0]
    tm = ROW_TILE
    rows = lambda n: pl.BlockSpec((tm, n), lambda i: (i, 0))

    def resident(shape):
        return pl.BlockSpec(shape, lambda i: (0,) * len(shape), pipeline_mode=pl.Buffered(1))

    return pl.pallas_call(
        _out_ffn_kernel,
        out_shape=jax.ShapeDtypeStruct((seq, D_MODEL), F32),
        grid=(seq // tm,),
        in_specs=[
            rows(D_MODEL), rows(GM_WIDTH), rows(DA_WIDTH), rows(CV_WIDTH),
            resident((D_MODEL, D_MODEL)),
            resident((1, D_MODEL)), resident((1, D_MODEL)), resident((1, D_MODEL)),
            resident((D_MODEL, 2 * D_FF)),
            resident((V7X_SUBLANES, 2 * D_FF)),
            resident((1, 2 * D_FF)),
            resident((D_FF, D_MODEL)),
        ],
        out_specs=rows(D_MODEL),
        scratch_shapes=[
            pltpu.VMEM((FFN_HALO, 2 * D_FF), F32),
            pltpu.VMEM((2, FFN_HALO + tm, FF_CHUNK), F32),
            pltpu.VMEM((2, FFN_HALO + tm, FF_CHUNK), F32),
            pltpu.VMEM((tm, D_MODEL), F32),
        ],
        compiler_params=pltpu.CompilerParams(
            dimension_semantics=("arbitrary",),
            vmem_limit_bytes=V7X_SCOPED_VMEM_BYTES),
        name="out_ffn",
    )(x, a, b, c, wo_bf, gpm, gpf, gqf, wu_bf, cw, cb, wd_bf)


def _bucket_table():
    n = np.arange(MAX_DISTANCE + 1)
    max_exact = N_BUCKETS // 2
    nf = np.maximum(n, 1).astype(np.float32)
    large = max_exact + (np.log(nf / max_exact) / math.log(MAX_DISTANCE / max_exact)
                         * (N_BUCKETS - max_exact)).astype(np.int32)
    large = np.minimum(large, N_BUCKETS - 1)
    return np.where(n < max_exact, n, large).astype(np.int32)


def _group_mean_matrix():
    g = np.arange(CV_WIDTH) // (CV_WIDTH // CV_GROUPS)
    return jnp.asarray((g[:, None] == g[None, :]) / (CV_WIDTH // CV_GROUPS), dtype=BF16)


def _pad_rows(w, rows):
    return jnp.concatenate([w, jnp.zeros((rows - w.shape[0], w.shape[1]), w.dtype)], axis=0)


def kernel(x, w_in, w_out, gm_ln_g, gm_ln_b, gm_w_s, gm_b_s, da_lq1, da_lk1, da_lq2, da_lk2, da_subln_g, rel_bias, cv_dw_w, cv_dw_b, cv_ln_g, cv_ln_b, ffn_w_up, ffn_conv_w, ffn_conv_b, ffn_w_down, pre_mix_g, post_mix_g, pre_ffn_g, post_ffn_g):
    batch, seq, d_model = x.shape
    depth = w_in.shape[0]
    assert batch == 1 and d_model == D_MODEL and w_in.shape[2] == IN_WIDTH
    assert seq % ROW_TILE == 0 and ROW_TILE == ATT_TILE and ATT_TILE >= MAX_DISTANCE
    assert ATT_Q_SLABS == 2 and seq % (ATT_Q_SLABS * ATT_TILE) == 0
    assert ROW_TILE % CHUNK == 0 and D_FF % FF_CHUNK == 0 and CV_HALO >= CV_KERNEL - 1

    gmat = _group_mean_matrix()
    row = lambda p: p.reshape(1, -1)

    xs = x[0]
    for l in range(depth):
        lambda_init = 0.8 - 0.6 * math.exp(-0.3 * l)
        bs_full = jnp.repeat(jnp.transpose(gm_b_s[l]), GM_HEAD_DIM, axis=1)
        w_bf = w_in[l].astype(BF16)
        q0, v0 = 2 * GM_WIDTH, 2 * GM_WIDTH + 2 * DA_WIDTH
        a, q, k, v, c = _mix_in(
            xs, row(pre_mix_g[l]), w_bf,
            jnp.transpose(w_bf[:, q0:q0 + DA_WIDTH]), jnp.transpose(w_bf[:, v0:v0 + DA_WIDTH]),
            row(gm_ln_g[l]), row(gm_ln_b[l]), gm_w_s[l], bs_full,
            _pad_rows(cv_dw_w[l], CV_HALO), row(cv_dw_b[l]), row(cv_ln_g[l]), row(cv_ln_b[l]), gmat)
        b = _diff_attn(rel_bias, q, k, v, row(da_lq1[l]), row(da_lk1[l]), row(da_lq2[l]), row(da_lk2[l]),
                       row(da_subln_g[l]), lambda_init)
        xs = _out_ffn(
            xs, a, b, c, w_out[l].astype(BF16),
            row(post_mix_g[l]), row(pre_ffn_g[l]), row(post_ffn_g[l]),
            ffn_w_up[l].astype(BF16), _pad_rows(ffn_conv_w[l], V7X_SUBLANES), row(ffn_conv_b[l]),
            ffn_w_down[l].astype(BF16))
    return xs[None]
```

```python
import functools
import math

import numpy as np
import jax
import jax.numpy as jnp
from jax import lax
from jax.experimental import pallas as pl
from jax.experimental.pallas import tpu as pltpu

F32 = jnp.float32
BF16 = jnp.bfloat16

D_MODEL = 1024
GM_HEADS = 4
GM_WIDTH = 256
GM_HEAD_DIM = GM_WIDTH // GM_HEADS
CHUNK = 128
DA_HEADS = 4
DA_WIDTH = 512
DA_V_DIM = 128
DA_QK_DIM = 64
CV_GROUPS = 4
CV_WIDTH = 256
CV_KERNEL = 31
D_FF = 2816
FFN_KERNEL = 3
N_BUCKETS = 32
MAX_DISTANCE = 128
EPS = 1e-6
IN_WIDTH = 2 * GM_WIDTH + 3 * DA_WIDTH + 2 * CV_WIDTH

V7X_SUBLANES = 8
V7X_LANES = 128
V7X_MXU_DIM = 256
V7X_SCOPED_VMEM_BYTES = 60000 * 1024

ROW_TILE = 512
ATT_TILE = 512
ATT_Q_SLABS = 2
ATT_GROUP = V7X_MXU_DIM
CV_HALO = 32
FFN_HALO = V7X_SUBLANES
FF_CHUNK = V7X_MXU_DIM
FFN_UP_SPLIT = 4
FFN_ROW_SPLIT = 4

LOG2E = math.log2(math.e)


_GELU_K1 = -2.0 * math.sqrt(2.0 / math.pi) * LOG2E
_GELU_K3 = _GELU_K1 * 0.044715


def _gelu(x):
    return x / (1.0 + jnp.exp2(x * (_GELU_K1 + _GELU_K3 * (x * x))))


def _rms(x, g):
    return x * lax.rsqrt(jnp.mean(x * x, axis=-1, keepdims=True) + EPS) * g


def _split_dot(x, g_ref):
    hi = x.astype(BF16)
    lo = (x - hi.astype(F32)).astype(BF16)
    g = g_ref[...]
    return (jnp.dot(hi, g, preferred_element_type=F32)
            + jnp.dot(lo, g, preferred_element_type=F32))


def _mix_in_kernel(x_ref, g_ref, w_ref, wqt_ref, wvt_ref, lng_ref, lnb_ref, ws_ref, bs_ref,
                   cw_ref, cb_ref, cg_ref, cbeta_ref, gmat_ref,
                   a_ref, qt_ref, k_ref, vt_ref, c_ref, hbuf):
    tm = x_ref.shape[0]
    i = pl.program_id(0)

    x = x_ref[...]
    hb = _rms(x, g_ref[...]).astype(BF16)

    def proj(c0, n):
        return jnp.dot(hb, w_ref[:, c0:c0 + n], preferred_element_type=F32)

    def proj_t(wt_ref):
        return lax.dot_general(wt_ref[...], hb, (((1,), (1,)), ((), ())), preferred_element_type=F32)

    c0 = 2 * GM_WIDTH

    cv = proj(c0 + 3 * DA_WIDTH, 2 * CV_WIDTH)
    hg = cv[:, :CV_WIDTH] * jax.nn.sigmoid(cv[:, CV_WIDTH:])

    @pl.when(i == 0)
    def _():
        hbuf[0, 0:CV_HALO, :] = jnp.zeros((CV_HALO, CV_WIDTH), F32)

    hbuf[0, CV_HALO:CV_HALO + tm, :] = hg
    n_shift_rows = tm + CV_HALO - V7X_SUBLANES
    for b in range(1, V7X_SUBLANES):
        hbuf[b, 0:n_shift_rows, :] = hbuf[0, b:b + n_shift_rows, :]

    gm = _gelu(proj(0, 2 * GM_WIDTH))
    qt = (proj_t(wqt_ref) * (DA_QK_DIM ** -0.5 * LOG2E)).astype(BF16)
    qt_ref[:, 0, :, :] = qt.reshape(DA_HEADS, DA_V_DIM, tm)
    u = gm[:, :GM_WIDTH]
    v = gm[:, GM_WIDTH:]
    mu = jnp.mean(v, axis=-1, keepdims=True)
    d = v - mu
    var = jnp.mean(d * d, axis=-1, keepdims=True)
    vn = d * lax.rsqrt(var + EPS) * lng_ref[...] + lnb_ref[...]

    row = lax.broadcasted_iota(jnp.int32, (CHUNK, CHUNK), 0)
    col = lax.broadcasted_iota(jnp.int32, (CHUNK, CHUNK), 1)
    causal = row >= col
    wcat = jnp.concatenate(
        [jnp.where(causal, ws_ref[h], 0.0).astype(BF16) for h in range(GM_HEADS)], axis=1)
    head_of_lane = lax.broadcasted_iota(jnp.int32, (CHUNK, GM_WIDTH), 1) // GM_HEAD_DIM
    bs = bs_ref[...]
    for c in range(tm // CHUNK):
        r0 = c * CHUNK
        vc = vn[r0:r0 + CHUNK, :]
        vstack = jnp.concatenate(
            [jnp.where(head_of_lane == h, vc, 0.0).astype(BF16) for h in range(GM_HEADS)], axis=0)
        mixed = jnp.dot(wcat, vstack, preferred_element_type=F32) + bs
        a_ref[r0:r0 + CHUNK, :] = (u[r0:r0 + CHUNK, :] * mixed).astype(BF16)

    k_ref[...] = proj(c0 + DA_WIDTH, DA_WIDTH).astype(BF16)
    vt_ref[:, 0, :, :] = proj_t(wvt_ref).astype(BF16).reshape(DA_HEADS, DA_V_DIM, tm)

    acc = jnp.broadcast_to(cb_ref[...], (tm, CV_WIDTH))
    for kk in range(CV_KERNEL):
        off = CV_HALO - (CV_KERNEL - 1) + kk
        b, a0 = off % V7X_SUBLANES, off - off % V7X_SUBLANES
        acc = acc + cw_ref[kk:kk + 1, :] * hbuf[b, a0:a0 + tm, :]
    hbuf[0, 0:CV_HALO, :] = hbuf[0, tm:tm + CV_HALO, :]

    gmu = _split_dot(acc, gmat_ref)
    dd = acc - gmu
    gvar = _split_dot(dd * dd, gmat_ref)
    y = dd * lax.rsqrt(gvar + EPS) * cg_ref[...] + cbeta_ref[...]
    c_ref[...] = (y * jax.nn.sigmoid(y)).astype(BF16)


def _mix_in(x, g, w_bf, wqt_bf, wvt_bf, lng, lnb, ws, bs_full, cw, cb, cg, cbeta, gmat):
    seq = x.shape[0]
    tm = ROW_TILE
    full = lambda shape: pl.BlockSpec(shape, lambda i: (0,) * len(shape))
    rows = lambda n: pl.BlockSpec((tm, n), lambda i: (i, 0))
    slabs = pl.BlockSpec((DA_HEADS, 1, DA_V_DIM, tm), lambda i: (0, i, 0, 0))
    slab_shape = jax.ShapeDtypeStruct((DA_HEADS, seq // tm, DA_V_DIM, tm), BF16)
    out_shapes = (
        jax.ShapeDtypeStruct((seq, GM_WIDTH), BF16),
        slab_shape,
        jax.ShapeDtypeStruct((seq, DA_WIDTH), BF16),
        slab_shape,
        jax.ShapeDtypeStruct((seq, CV_WIDTH), BF16),
    )
    return pl.pallas_call(
        _mix_in_kernel,
        out_shape=out_shapes,
        grid=(seq // tm,),
        in_specs=[
            rows(D_MODEL),
            full((1, D_MODEL)),
            full((D_MODEL, IN_WIDTH)),
            full((DA_WIDTH, D_MODEL)), full((DA_WIDTH, D_MODEL)),
            full((1, GM_WIDTH)), full((1, GM_WIDTH)),
            full((GM_HEADS, CHUNK, CHUNK)),
            full((CHUNK, GM_WIDTH)),
            full((CV_HALO, CV_WIDTH)),
            full((1, CV_WIDTH)), full((1, CV_WIDTH)), full((1, CV_WIDTH)),
            full((CV_WIDTH, CV_WIDTH)),
        ],
        out_specs=(rows(GM_WIDTH), slabs, rows(DA_WIDTH), slabs, rows(CV_WIDTH)),
        scratch_shapes=[pltpu.VMEM((V7X_SUBLANES, CV_HALO + tm, CV_WIDTH), F32)],
        compiler_params=pltpu.CompilerParams(
            dimension_semantics=("arbitrary",),
            vmem_limit_bytes=V7X_SCOPED_VMEM_BYTES),
        name="mix_in",
    )(x, g, w_bf, wqt_bf, wvt_bf, lng, lnb, ws, bs_full, cw, cb, cg, cbeta, gmat)


def _fill_bias(rb_ref, bias_s, head, t):
    sub = V7X_LANES
    table = _bucket_table()
    first_rel = [int(np.argmax(table == b)) for b in range(N_BUCKETS)]
    last = rb_ref[N_BUCKETS - 1, head]
    vals = [(rb_ref[b, head] - last) * LOG2E for b in range(N_BUCKETS)]
    d0 = (lax.broadcasted_iota(jnp.int32, (sub, sub), 1)
          - lax.broadcasted_iota(jnp.int32, (sub, sub), 0))

    def band(base):
        rel = d0 + base
        v = jnp.full((sub, sub), vals[0], F32)
        for b in range(1, N_BUCKETS):
            v = jnp.where(rel >= first_rel[b], vals[b], v)
        return jnp.where(rel < 0, -jnp.inf, v)

    bands = {0: band(0), sub: band(sub)}
    for cb in range(bias_s.shape[0] // sub):
        for rb in range(bias_s.shape[1] // sub):
            base = sub * (rb - cb) + t
            if base in bands:
                tile = bands[base]
            elif base > sub:
                tile = jnp.zeros((sub, sub), F32)
            else:
                tile = jnp.full((sub, sub), -jnp.inf, F32)
            bias_s[cb * sub:(cb + 1) * sub, rb * sub:(rb + 1) * sub] = tile


def _diff_attn_kernel(rb_ref, qt_ref, k_ref, vt_ref, lq1_ref, lk1_ref, lq2_ref, lk2_ref, sg_ref,
                      o_ref, qst, acc, m_s, l_s, s_buf, mx_buf, bias_s, *, lambda_init):
    t = ATT_TILE
    tq = o_ref.shape[0]
    i = pl.program_id(1)

    @pl.when(i == 0)
    def _():
        _fill_bias(rb_ref, bias_s, pl.program_id(0), t)

    for slab in range(ATT_Q_SLABS):
        qt = qt_ref[0, slab]
        chan = lax.broadcasted_iota(jnp.int32, qt.shape, 0)
        zero = jnp.zeros_like(qt)
        qst[:, slab * t:(slab + 1) * t] = jnp.where(chan < DA_QK_DIM, qt, zero)
        qst[:, tq + slab * t:tq + (slab + 1) * t] = jnp.where(chan >= DA_QK_DIM, qt, zero)
    m_s[...] = jnp.full(m_s.shape, -jnp.inf, F32)
    l_s[...] = jnp.zeros(l_s.shape, F32)
    acc[...] = jnp.zeros(acc.shape, F32)

    n_groups = 2 * tq // ATT_GROUP
    cols = [slice(g * ATT_GROUP, (g + 1) * ATT_GROUP) for g in range(n_groups)]

    def group_kind(g, d):
        r0 = (g * ATT_GROUP) % tq
        rel_min = r0 + t * (1 - d) - (t - 1)
        rel_max = r0 + ATT_GROUP - 1 + t * (1 - d)
        if rel_max < 0:
            return "masked"
        return "plain" if rel_min >= MAX_DISTANCE else "biased"

    def scores(j, slot, d=None):
        kb = k_ref[pl.ds(pl.multiple_of(j * t, t), t), :]
        for g in range(n_groups):
            if d is not None and group_kind(g, d) == "masked":
                continue
            s = jnp.dot(kb, qst[:, cols[g]], preferred_element_type=F32)
            s_buf[slot, g] = s
            mx_buf[slot, :, cols[g]] = jnp.max(s, axis=0, keepdims=True)

    def softmax_pv(j, slot, d=None):
        for g in range(n_groups):
            kind = "plain" if d is None else group_kind(g, d)
            if kind == "masked":
                continue
            s = s_buf[slot, g]
            if kind == "plain":
                s_max = mx_buf[slot, :, cols[g]]
            else:
                r0 = (g * ATT_GROUP) % tq
                s = s + bias_s[d * t:(d + 1) * t, r0:r0 + ATT_GROUP]
                s_max = jnp.max(s, axis=0, keepdims=True)
            m_prev = m_s[:, cols[g]]
            m_new = jnp.maximum(m_prev, s_max)
            alpha = jnp.exp2(m_prev - m_new)
            p = jnp.exp2(s - m_new)
            l_s[:, cols[g]] = alpha * l_s[:, cols[g]] + jnp.sum(p, axis=0, keepdims=True)
            acc[:, cols[g]] = (alpha * acc[:, cols[g]]
                               + jnp.dot(vt_ref[0, j], p.astype(BF16), preferred_element_type=F32))
            m_s[:, cols[g]] = m_new

    first = ATT_Q_SLABS * i
    scores(0, 0)

    def pair(j, d_second):
        scores(j + 1, 1)
        softmax_pv(j, 0)
        scores(j + 2, 0)
        softmax_pv(j + 1, 1, d_second)

    def far_pair(pp, carry):
        pair(2 * pp, None)
        return carry

    lax.fori_loop(0, i - 1, far_pair, 0)

    @pl.when(i > 0)
    def _():
        pair(first - 2, 0)

    scores(first + 1, 1, 2)
    softmax_pv(first, 0, 1)
    softmax_pv(first + 1, 1, 2)

    o1 = acc[:, 0:tq] / l_s[:, 0:tq]
    o2 = acc[:, tq:2 * tq] / l_s[:, tq:2 * tq]
    lam = (jnp.exp(jnp.sum(lq1_ref[...] * lk1_ref[...], axis=-1, keepdims=True))
           - jnp.exp(jnp.sum(lq2_ref[...] * lk2_ref[...], axis=-1, keepdims=True))
           + lambda_init)
    ot = o1 - lam * o2
    ot = ot * lax.rsqrt(jnp.mean(ot * ot, axis=0, keepdims=True) + EPS)
    o_ref[...] = (jnp.transpose(ot) * sg_ref[...] * (1.0 - lambda_init)).astype(BF16)


def _diff_attn(rel_bias, qt, k, vt, lq1, lk1, lq2, lk2, sg, lambda_init):
    seq = k.shape[0]
    t = ATT_TILE
    tq = ATT_Q_SLABS * t
    vec = lambda n: pl.BlockSpec((1, n), lambda h, i: (0, 0))
    return pl.pallas_call(
        functools.partial(_diff_attn_kernel, lambda_init=lambda_init),
        out_shape=jax.ShapeDtypeStruct((seq, DA_WIDTH), BF16),
        grid=(DA_HEADS, seq // tq),
        in_specs=[
            pl.BlockSpec(memory_space=pltpu.SMEM),
            pl.BlockSpec((1, ATT_Q_SLABS, DA_V_DIM, t), lambda h, i: (h, i, 0, 0)),
            pl.BlockSpec((seq, DA_V_DIM), lambda h, i: (0, h)),
            pl.BlockSpec((1, seq // t, DA_V_DIM, t), lambda h, i: (h, 0, 0, 0)),
            vec(DA_QK_DIM), vec(DA_QK_DIM), vec(DA_QK_DIM), vec(DA_QK_DIM),
            vec(DA_V_DIM),
        ],
        out_specs=pl.BlockSpec((tq, DA_V_DIM), lambda h, i: (i, h)),
        scratch_shapes=[
            pltpu.VMEM((2 * DA_QK_DIM, 2 * tq), BF16),
            pltpu.VMEM((DA_V_DIM, 2 * tq), F32),
            pltpu.VMEM((1, 2 * tq), F32),
            pltpu.VMEM((1, 2 * tq), F32),
            pltpu.VMEM((2, 2 * tq // ATT_GROUP, t, ATT_GROUP), F32),
            pltpu.VMEM((2, 1, 2 * tq), F32),
            pltpu.VMEM(((ATT_Q_SLABS + 1) * t, tq), F32),
        ],
        compiler_params=pltpu.CompilerParams(
            dimension_semantics=("arbitrary", "arbitrary"),
            vmem_limit_bytes=V7X_SCOPED_VMEM_BYTES),
        name="diff_attn",
    )(rel_bias, qt, k, vt, lq1, lk1, lq2, lk2, sg)


def _out_ffn_kernel(x_ref, a_ref, b_ref, c_ref, wo_ref, gpm_ref, gpf_ref, gqf_ref,
                    wu_ref, cw_ref, cb_ref, wd_ref, xo_ref, carry, ubuf0, ubuf1, yacc):
    tm = x_ref.shape[0]
    i = pl.program_id(0)
    n_chunks = D_FF // FF_CHUNK

    @pl.when(i == 0)
    def _():
        carry[...] = jnp.zeros(carry.shape, F32)

    o_a, o_b, o_c = 0, GM_WIDTH, GM_WIDTH + DA_WIDTH
    mix = (jnp.dot(a_ref[...], wo_ref[o_a:o_b, :], preferred_element_type=F32)
           + jnp.dot(b_ref[...], wo_ref[o_b:o_c, :], preferred_element_type=F32)
           + jnp.dot(c_ref[...], wo_ref[o_c:, :], preferred_element_type=F32))
    x1 = x_ref[...] + _rms(mix, gpm_ref[...])
    hb = _rms(x1, gpf_ref[...]).astype(BF16)

    ub = tm // FFN_UP_SPLIT
    rb = tm // FFN_ROW_SPLIT

    def up_proj(ch, r):
        bufs = (ubuf0, ubuf1)[ch % 2]
        r0 = r * ub
        for half in range(2):
            col = half * D_FF + ch * FF_CHUNK
            up = jnp.dot(hb[r0:r0 + ub, :], wu_ref[:, col:col + FF_CHUNK], preferred_element_type=F32)
            if r == 0:
                bufs[half, 0:FFN_HALO, :] = carry[:, col:col + FF_CHUNK]
            bufs[half, FFN_HALO + r0:FFN_HALO + r0 + ub, :] = up
            if r == FFN_UP_SPLIT - 1:
                carry[:, col:col + FF_CHUNK] = up[ub - FFN_HALO:ub, :]

    def conv(ch, half, r):
        bufs = (ubuf0, ubuf1)[ch % 2]
        col = half * D_FF + ch * FF_CHUNK
        out = cb_ref[:, col:col + FF_CHUNK]
        for kk in range(FFN_KERNEL):
            off = FFN_HALO - (FFN_KERNEL - 1) + kk + r * rb
            out = out + cw_ref[kk:kk + 1, col:col + FF_CHUNK] * bufs[half, off:off + rb, :]
        return out

    for r in range(FFN_UP_SPLIT):
        up_proj(0, r)
    for ch in range(n_chunks):
        col = ch * FF_CHUNK
        for r in range(FFN_ROW_SPLIT):
            if ch + 1 < n_chunks and r % (FFN_ROW_SPLIT // FFN_UP_SPLIT) == 0:
                up_proj(ch + 1, r // (FFN_ROW_SPLIT // FFN_UP_SPLIT))
            act = (_gelu(conv(ch, 0, r)) * conv(ch, 1, r)).astype(BF16)
            contrib = jnp.dot(act, wd_ref[col:col + FF_CHUNK, :], preferred_element_type=F32)
            rows = slice(r * rb, (r + 1) * rb)
            if ch == 0:
                yacc[rows, :] = contrib
            else:
                yacc[rows, :] += contrib

    xo_ref[...] = x1 + _rms(yacc[...], gqf_ref[...])


def _out_ffn(x, a, b, c, wo_bf, gpm, gpf, gqf, wu_bf, cw, cb, wd_bf):
    seq = x.shape[0]
    tm = ROW_TILE
    rows = lambda n: pl.BlockSpec((tm, n), lambda i: (i, 0))

    def resident(shape):
        return pl.BlockSpec(shape, lambda i: (0,) * len(shape), pipeline_mode=pl.Buffered(1))

    return pl.pallas_call(
        _out_ffn_kernel,
        out_shape=jax.ShapeDtypeStruct((seq, D_MODEL), F32),
        grid=(seq // tm,),
        in_specs=[
            rows(D_MODEL), rows(GM_WIDTH), rows(DA_WIDTH), rows(CV_WIDTH),
            resident((D_MODEL, D_MODEL)),
            resident((1, D_MODEL)), resident((1, D_MODEL)), resident((1, D_MODEL)),
            resident((D_MODEL, 2 * D_FF)),
            resident((V7X_SUBLANES, 2 * D_FF)),
            resident((1, 2 * D_FF)),
            resident((D_FF, D_MODEL)),
        ],
        out_specs=rows(D_MODEL),
        scratch_shapes=[
            pltpu.VMEM((FFN_HALO, 2 * D_FF), F32),
            pltpu.VMEM((2, FFN_HALO + tm, FF_CHUNK), F32),
            pltpu.VMEM((2, FFN_HALO + tm, FF_CHUNK), F32),
            pltpu.VMEM((tm, D_MODEL), F32),
        ],
        compiler_params=pltpu.CompilerParams(
            dimension_semantics=("arbitrary",),
            vmem_limit_bytes=V7X_SCOPED_VMEM_BYTES),
        name="out_ffn",
    )(x, a, b, c, wo_bf, gpm, gpf, gqf, wu_bf, cw, cb, wd_bf)


def _bucket_table():
    n = np.arange(MAX_DISTANCE + 1)
    max_exact = N_BUCKETS // 2
    nf = np.maximum(n, 1).astype(np.float32)
    large = max_exact + (np.log(nf / max_exact) / math.log(MAX_DISTANCE / max_exact)
                         * (N_BUCKETS - max_exact)).astype(np.int32)
    large = np.minimum(large, N_BUCKETS - 1)
    return np.where(n < max_exact, n, large).astype(np.int32)


def _group_mean_matrix():
    g = np.arange(CV_WIDTH) // (CV_WIDTH // CV_GROUPS)
    return jnp.asarray((g[:, None] == g[None, :]) / (CV_WIDTH // CV_GROUPS), dtype=BF16)


def _pad_rows(w, rows):
    return jnp.concatenate([w, jnp.zeros((rows - w.shape[0], w.shape[1]), w.dtype)], axis=0)


def kernel(x, w_in, w_out, gm_ln_g, gm_ln_b, gm_w_s, gm_b_s, da_lq1, da_lk1, da_lq2, da_lk2, da_subln_g, rel_bias, cv_dw_w, cv_dw_b, cv_ln_g, cv_ln_b, ffn_w_up, ffn_conv_w, ffn_conv_b, ffn_w_down, pre_mix_g, post_mix_g, pre_ffn_g, post_ffn_g):
    batch, seq, d_model = x.shape
    depth = w_in.shape[0]
    assert batch == 1 and d_model == D_MODEL and w_in.shape[2] == IN_WIDTH
    assert seq % ROW_TILE == 0 and ROW_TILE == ATT_TILE and ATT_TILE >= MAX_DISTANCE
    assert ATT_Q_SLABS == 2 and seq % (ATT_Q_SLABS * ATT_TILE) == 0
    assert ROW_TILE % CHUNK == 0 and D_FF % FF_CHUNK == 0 and CV_HALO >= CV_KERNEL - 1

    gmat = _group_mean_matrix()
    row = lambda p: p.reshape(1, -1)

    xs = x[0]
    for l in range(depth):
        lambda_init = 0.8 - 0.6 * math.exp(-0.3 * l)
        bs_full = jnp.repeat(jnp.transpose(gm_b_s[l]), GM_HEAD_DIM, axis=1)
        w_bf = w_in[l].astype(BF16)
        q0, v0 = 2 * GM_WIDTH, 2 * GM_WIDTH + 2 * DA_WIDTH
        a, q, k, v, c = _mix_in(
            xs, row(pre_mix_g[l]), w_bf,
            jnp.transpose(w_bf[:, q0:q0 + DA_WIDTH]), jnp.transpose(w_bf[:, v0:v0 + DA_WIDTH]),
            row(gm_ln_g[l]), row(gm_ln_b[l]), gm_w_s[l], bs_full,
            _pad_rows(cv_dw_w[l], CV_HALO), row(cv_dw_b[l]), row(cv_ln_g[l]), row(cv_ln_b[l]), gmat)
        b = _diff_attn(rel_bias, q, k, v, row(da_lq1[l]), row(da_lk1[l]), row(da_lq2[l]), row(da_lk2[l]),
                       row(da_subln_g[l]), lambda_init)
        xs = _out_ffn(
            xs, a, b, c, w_out[l].astype(BF16),
            row(post_mix_g[l]), row(pre_ffn_g[l]), row(post_ffn_g[l]),
            ffn_w_up[l].astype(BF16), _pad_rows(ffn_conv_w[l], V7X_SUBLANES), row(ffn_conv_b[l]),
            ffn_w_down[l].astype(BF16))
    return xs[None]
```

```python
import functools
import math

import numpy as np
import jax
import jax.numpy as jnp
from jax import lax
from jax.experimental import pallas as pl
from jax.experimental.pallas import tpu as pltpu

F32 = jnp.float32
BF16 = jnp.bfloat16

D_MODEL = 1024
GM_HEADS = 4
GM_WIDTH = 256
GM_HEAD_DIM = GM_WIDTH // GM_HEADS
CHUNK = 128
DA_HEADS = 4
DA_WIDTH = 512
DA_V_DIM = 128
DA_QK_DIM = 64
CV_GROUPS = 4
CV_WIDTH = 256
CV_KERNEL = 31
D_FF = 2816
FFN_KERNEL = 3
N_BUCKETS = 32
MAX_DISTANCE = 128
EPS = 1e-6
IN_WIDTH = 2 * GM_WIDTH + 3 * DA_WIDTH + 2 * CV_WIDTH

V7X_SUBLANES = 8
V7X_LANES = 128
V7X_MXU_DIM = 256
V7X_SCOPED_VMEM_BYTES = 60000 * 1024

ROW_TILE = 512
ATT_TILE = 512
ATT_Q_SLABS = 2
ATT_GROUP = V7X_MXU_DIM
CV_HALO = 32
FFN_HALO = V7X_SUBLANES
FF_CHUNK = V7X_MXU_DIM
FFN_UP_SPLIT = 2
FFN_ROW_SPLIT = 2

LOG2E = math.log2(math.e)


_GELU_K1 = -2.0 * math.sqrt(2.0 / math.pi) * LOG2E
_GELU_K3 = _GELU_K1 * 0.044715


def _gelu(x):
    return x / (1.0 + jnp.exp2(x * (_GELU_K1 + _GELU_K3 * (x * x))))


def _rms(x, g):
    return x * lax.rsqrt(jnp.mean(x * x, axis=-1, keepdims=True) + EPS) * g


def _split_dot(x, g_ref):
    hi = x.astype(BF16)
    lo = (x - hi.astype(F32)).astype(BF16)
    g = g_ref[...]
    return (jnp.dot(hi, g, preferred_element_type=F32)
            + jnp.dot(lo, g, preferred_element_type=F32))


def _mix_in_kernel(x_ref, g_ref, w_ref, wqt_ref, wvt_ref, lng_ref, lnb_ref, ws_ref, bs_ref,
                   cw_ref, cb_ref, cg_ref, cbeta_ref, gmat_ref,
                   a_ref, qt_ref, k_ref, vt_ref, c_ref, hbuf):
    tm = x_ref.shape[0]
    i = pl.program_id(0)

    x = x_ref[...]
    hb = _rms(x, g_ref[...]).astype(BF16)

    def proj(c0, n):
        return jnp.dot(hb, w_ref[0, :, c0:c0 + n], preferred_element_type=F32)

    def proj_t(wt_ref):
        return lax.dot_general(wt_ref[0], hb, (((1,), (1,)), ((), ())), preferred_element_type=F32)

    c0 = 2 * GM_WIDTH

    cv = proj(c0 + 3 * DA_WIDTH, 2 * CV_WIDTH)
    hg = cv[:, :CV_WIDTH] * jax.nn.sigmoid(cv[:, CV_WIDTH:])

    @pl.when(i == 0)
    def _():
        hbuf[0, 0:CV_HALO, :] = jnp.zeros((CV_HALO, CV_WIDTH), F32)

    hbuf[0, CV_HALO:CV_HALO + tm, :] = hg
    n_shift_rows = tm + CV_HALO - V7X_SUBLANES
    for b in range(1, V7X_SUBLANES):
        hbuf[b, 0:n_shift_rows, :] = hbuf[0, b:b + n_shift_rows, :]

    gm = _gelu(proj(0, 2 * GM_WIDTH))
    qt = (proj_t(wqt_ref) * (DA_QK_DIM ** -0.5 * LOG2E)).astype(BF16)
    qt_ref[:, 0, :, :] = qt.reshape(DA_HEADS, DA_V_DIM, tm)
    u = gm[:, :GM_WIDTH]
    v = gm[:, GM_WIDTH:]
    mu = jnp.mean(v, axis=-1, keepdims=True)
    d = v - mu
    var = jnp.mean(d * d, axis=-1, keepdims=True)
    vn = d * lax.rsqrt(var + EPS) * lng_ref[...] + lnb_ref[...]

    row = lax.broadcasted_iota(jnp.int32, (CHUNK, CHUNK), 0)
    col = lax.broadcasted_iota(jnp.int32, (CHUNK, CHUNK), 1)
    causal = row >= col
    wcat = jnp.concatenate(
        [jnp.where(causal, ws_ref[h], 0.0).astype(BF16) for h in range(GM_HEADS)], axis=1)
    head_of_lane = lax.broadcasted_iota(jnp.int32, (CHUNK, GM_WIDTH), 1) // GM_HEAD_DIM
    bs = bs_ref[...]
    for c in range(tm // CHUNK):
        r0 = c * CHUNK
        vc = vn[r0:r0 + CHUNK, :]
        vstack = jnp.concatenate(
            [jnp.where(head_of_lane == h, vc, 0.0).astype(BF16) for h in range(GM_HEADS)], axis=0)
        mixed = jnp.dot(wcat, vstack, preferred_element_type=F32) + bs
        a_ref[r0:r0 + CHUNK, :] = (u[r0:r0 + CHUNK, :] * mixed).astype(BF16)

    k_ref[...] = proj(c0 + DA_WIDTH, DA_WIDTH).astype(BF16)
    vt = proj_t(wvt_ref).astype(BF16).reshape(DA_HEADS, DA_V_DIM, tm)
    for kc in range(tm // V7X_LANES):
        vt_ref[:, 0, kc, :, :] = vt[:, :, kc * V7X_LANES:(kc + 1) * V7X_LANES]

    acc = jnp.broadcast_to(cb_ref[...], (tm, CV_WIDTH))
    for kk in range(CV_KERNEL):
        off = CV_HALO - (CV_KERNEL - 1) + kk
        b, a0 = off % V7X_SUBLANES, off - off % V7X_SUBLANES
        acc = acc + cw_ref[kk:kk + 1, :] * hbuf[b, a0:a0 + tm, :]
    hbuf[0, 0:CV_HALO, :] = hbuf[0, tm:tm + CV_HALO, :]

    gmu = _split_dot(acc, gmat_ref)
    dd = acc - gmu
    gvar = _split_dot(dd * dd, gmat_ref)
    y = dd * lax.rsqrt(gvar + EPS) * cg_ref[...] + cbeta_ref[...]
    c_ref[...] = (y * jax.nn.sigmoid(y)).astype(BF16)


def _mix_in(layer, x, g, w_bf, wqt_bf, wvt_bf, lng, lnb, ws, bs_full, cw, cb, cg, cbeta, gmat):
    seq = x.shape[0]
    tm = ROW_TILE
    full = lambda shape: pl.BlockSpec(shape, lambda i: (0,) * len(shape))
    of_layer = lambda shape: pl.BlockSpec((1,) + shape, lambda i: (layer,) + (0,) * len(shape))
    rows = lambda n: pl.BlockSpec((tm, n), lambda i: (i, 0))
    slabs = pl.BlockSpec((DA_HEADS, 1, DA_V_DIM, tm), lambda i: (0, i, 0, 0))
    slab_shape = jax.ShapeDtypeStruct((DA_HEADS, seq // tm, DA_V_DIM, tm), BF16)
    pieces = pl.BlockSpec((DA_HEADS, 1, tm // V7X_LANES, DA_V_DIM, V7X_LANES), lambda i: (0, i, 0, 0, 0))
    pieces_shape = jax.ShapeDtypeStruct((DA_HEADS, seq // tm, tm // V7X_LANES, DA_V_DIM, V7X_LANES), BF16)
    out_shapes = (
        jax.ShapeDtypeStruct((seq, GM_WIDTH), BF16),
        slab_shape,
        jax.ShapeDtypeStruct((seq, DA_WIDTH), BF16),
        pieces_shape,
        jax.ShapeDtypeStruct((seq, CV_WIDTH), BF16),
    )
    return pl.pallas_call(
        _mix_in_kernel,
        out_shape=out_shapes,
        grid=(seq // tm,),
        in_specs=[
            rows(D_MODEL),
            full((1, D_MODEL)),
            of_layer((D_MODEL, IN_WIDTH)),
            of_layer((DA_WIDTH, D_MODEL)), of_layer((DA_WIDTH, D_MODEL)),
            full((1, GM_WIDTH)), full((1, GM_WIDTH)),
            full((GM_HEADS, CHUNK, CHUNK)),
            full((CHUNK, GM_WIDTH)),
            full((CV_HALO, CV_WIDTH)),
            full((1, CV_WIDTH)), full((1, CV_WIDTH)), full((1, CV_WIDTH)),
            full((CV_WIDTH, CV_WIDTH)),
        ],
        out_specs=(rows(GM_WIDTH), slabs, rows(DA_WIDTH), pieces, rows(CV_WIDTH)),
        scratch_shapes=[pltpu.VMEM((V7X_SUBLANES, CV_HALO + tm, CV_WIDTH), F32)],
        compiler_params=pltpu.CompilerParams(
            dimension_semantics=("arbitrary",),
            vmem_limit_bytes=V7X_SCOPED_VMEM_BYTES),
        name="mix_in",
    )(x, g, w_bf, wqt_bf, wvt_bf, lng, lnb, ws, bs_full, cw, cb, cg, cbeta, gmat)


def _fill_bias(rb_ref, bias_s, head, t):
    sub = V7X_LANES
    table = _bucket_table()
    first_rel = [int(np.argmax(table == b)) for b in range(N_BUCKETS)]
    last = rb_ref[N_BUCKETS - 1, head]
    vals = [(rb_ref[b, head] - last) * LOG2E for b in range(N_BUCKETS)]
    d0 = (lax.broadcasted_iota(jnp.int32, (sub, sub), 1)
          - lax.broadcasted_iota(jnp.int32, (sub, sub), 0))

    def band(base):
        rel = d0 + base
        v = jnp.full((sub, sub), vals[0], F32)
        for b in range(1, N_BUCKETS):
            v = jnp.where(rel >= first_rel[b], vals[b], v)
        return jnp.where(rel < 0, -jnp.inf, v)

    bands = {0: band(0), sub: band(sub)}
    for cb in range(bias_s.shape[0] // sub):
        for rb in range(bias_s.shape[1] // sub):
            base = sub * (rb - cb) + t
            if base in bands:
                tile = bands[base]
            elif base > sub:
                tile = jnp.zeros((sub, sub), F32)
            else:
                tile = jnp.full((sub, sub), -jnp.inf, F32)
            bias_s[cb * sub:(cb + 1) * sub, rb * sub:(rb + 1) * sub] = tile


def _diff_attn_kernel(rb_ref, qt_ref, k_ref, vt_ref, lq1_ref, lk1_ref, lq2_ref, lk2_ref, sg_ref,
                      o_ref, qst, acc, m_s, l_s, s_buf, mx_buf, bias_s, *, lambda_init):
    t = ATT_TILE
    tq = o_ref.shape[0]
    i = pl.program_id(1)

    @pl.when(i == 0)
    def _():
        _fill_bias(rb_ref, bias_s, pl.program_id(0), t)

    n_groups = 2 * tq // ATT_GROUP
    cols = [slice(g * ATT_GROUP, (g + 1) * ATT_GROUP) for g in range(n_groups)]
    groups_per_slab = t // ATT_GROUP
    for slab in range(ATT_Q_SLABS):
        qt = qt_ref[0, slab]
        chan = lax.broadcasted_iota(jnp.int32, qt.shape, 0)
        zero = jnp.zeros_like(qt)
        q1 = jnp.where(chan < DA_QK_DIM, qt, zero)
        q2 = jnp.where(chan >= DA_QK_DIM, qt, zero)
        for gg in range(groups_per_slab):
            g = slab * groups_per_slab + gg
            qst[g] = q1[:, gg * ATT_GROUP:(gg + 1) * ATT_GROUP]
            qst[n_groups // 2 + g] = q2[:, gg * ATT_GROUP:(gg + 1) * ATT_GROUP]
    m_s[...] = jnp.full(m_s.shape, -jnp.inf, F32)
    l_s[...] = jnp.zeros(l_s.shape, F32)
    acc[...] = jnp.zeros(acc.shape, F32)

    def group_kind(g, d):
        r0 = (g * ATT_GROUP) % tq
        rel_min = r0 + t * (1 - d) - (t - 1)
        rel_max = r0 + ATT_GROUP - 1 + t * (1 - d)
        if rel_max < 0:
            return "masked"
        return "plain" if rel_min >= MAX_DISTANCE else "biased"

    def scores(j, slot, d=None):
        kb = k_ref[pl.ds(pl.multiple_of(j * t, t), t), :]
        for g in range(n_groups):
            if d is not None and group_kind(g, d) == "masked":
                continue
            s = jnp.dot(kb, qst[g], preferred_element_type=F32)
            s_buf[slot, g] = s
            mx_buf[slot, :, cols[g]] = jnp.max(s, axis=0, keepdims=True)

    def softmax_pv(j, slot, d=None):
        vt = jnp.concatenate([vt_ref[0, j, kc] for kc in range(t // V7X_LANES)], axis=1)
        for g in range(n_groups):
            kind = "plain" if d is None else group_kind(g, d)
            if kind == "masked":
                continue
            s = s_buf[slot, g]
            if kind == "plain":
                s_max = mx_buf[slot, :, cols[g]]
            else:
                r0 = (g * ATT_GROUP) % tq
                s = s + bias_s[d * t:(d + 1) * t, r0:r0 + ATT_GROUP]
                s_max = jnp.max(s, axis=0, keepdims=True)
            m_prev = m_s[:, cols[g]]
            m_new = jnp.maximum(m_prev, s_max)
            alpha = jnp.exp2(m_prev - m_new)
            p = jnp.exp2(s - m_new)
            l_s[:, cols[g]] = alpha * l_s[:, cols[g]] + jnp.sum(p, axis=0, keepdims=True)
            acc[g] = alpha * acc[g] + jnp.dot(vt, p.astype(BF16), preferred_element_type=F32)
            m_s[:, cols[g]] = m_new

    first = ATT_Q_SLABS * i
    scores(0, 0)

    def pair(j, d_second):
        scores(j + 1, 1)
        softmax_pv(j, 0)
        scores(j + 2, 0)
        softmax_pv(j + 1, 1, d_second)

    def far_pair(pp, carry):
        pair(2 * pp, None)
        return carry

    lax.fori_loop(0, i - 1, far_pair, 0)

    @pl.when(i > 0)
    def _():
        pair(first - 2, 0)

    scores(first + 1, 1, 2)
    softmax_pv(first, 0, 1)
    softmax_pv(first + 1, 1, 2)

    half = n_groups // 2
    o1 = jnp.concatenate([acc[g] for g in range(half)], axis=1) / l_s[:, 0:tq]
    o2 = jnp.concatenate([acc[half + g] for g in range(half)], axis=1) / l_s[:, tq:2 * tq]
    lam = (jnp.exp(jnp.sum(lq1_ref[...] * lk1_ref[...], axis=-1, keepdims=True))
           - jnp.exp(jnp.sum(lq2_ref[...] * lk2_ref[...], axis=-1, keepdims=True))
           + lambda_init)
    ot = o1 - lam * o2
    ot = ot * lax.rsqrt(jnp.mean(ot * ot, axis=0, keepdims=True) + EPS)
    o_ref[...] = (jnp.transpose(ot) * sg_ref[...] * (1.0 - lambda_init)).astype(BF16)


def _diff_attn(rel_bias, qt, k, vt, lq1, lk1, lq2, lk2, sg, lambda_init):
    seq = k.shape[0]
    t = ATT_TILE
    tq = ATT_Q_SLABS * t
    vec = lambda n: pl.BlockSpec((1, n), lambda h, i: (0, 0))
    return pl.pallas_call(
        functools.partial(_diff_attn_kernel, lambda_init=lambda_init),
        out_shape=jax.ShapeDtypeStruct((seq, DA_WIDTH), BF16),
        grid=(DA_HEADS, seq // tq),
        in_specs=[
            pl.BlockSpec(memory_space=pltpu.SMEM),
            pl.BlockSpec((1, ATT_Q_SLABS, DA_V_DIM, t), lambda h, i: (h, i, 0, 0)),
            pl.BlockSpec((seq, DA_V_DIM), lambda h, i: (0, h)),
            pl.BlockSpec((1, seq // t, t // V7X_LANES, DA_V_DIM, V7X_LANES), lambda h, i: (h, 0, 0, 0, 0)),
            vec(DA_QK_DIM), vec(DA_QK_DIM), vec(DA_QK_DIM), vec(DA_QK_DIM),
            vec(DA_V_DIM),
        ],
        out_specs=pl.BlockSpec((tq, DA_V_DIM), lambda h, i: (i, h)),
        scratch_shapes=[
            pltpu.VMEM((2 * tq // ATT_GROUP, 2 * DA_QK_DIM, ATT_GROUP), BF16),
            pltpu.VMEM((2 * tq // ATT_GROUP, DA_V_DIM, ATT_GROUP), F32),
            pltpu.VMEM((1, 2 * tq), F32),
            pltpu.VMEM((1, 2 * tq), F32),
            pltpu.VMEM((2, 2 * tq // ATT_GROUP, t, ATT_GROUP), F32),
            pltpu.VMEM((2, 1, 2 * tq), F32),
            pltpu.VMEM(((ATT_Q_SLABS + 1) * t, tq), F32),
        ],
        compiler_params=pltpu.CompilerParams(
            dimension_semantics=("arbitrary", "arbitrary"),
            vmem_limit_bytes=V7X_SCOPED_VMEM_BYTES),
        name="diff_attn",
    )(rel_bias, qt, k, vt, lq1, lk1, lq2, lk2, sg)


def _out_ffn_kernel(x_ref, a_ref, b_ref, c_ref, wo_ref, gpm_ref, gpf_ref, gqf_ref,
                    wu_ref, cw_ref, cb_ref, wd_ref, xo_ref, carry, ubuf0, ubuf1, yacc):
    tm = x_ref.shape[0]
    i = pl.program_id(0)
    n_chunks = D_FF // FF_CHUNK

    @pl.when(i == 0)
    def _():
        carry[...] = jnp.zeros(carry.shape, F32)

    o_a, o_b, o_c = 0, GM_WIDTH, GM_WIDTH + DA_WIDTH
    mix = (jnp.dot(a_ref[...], wo_ref[0, o_a:o_b, :], preferred_element_type=F32)
           + jnp.dot(b_ref[...], wo_ref[0, o_b:o_c, :], preferred_element_type=F32)
           + jnp.dot(c_ref[...], wo_ref[0, o_c:, :], preferred_element_type=F32))
    x1 = x_ref[...] + _rms(mix, gpm_ref[...])
    hb = _rms(x1, gpf_ref[...]).astype(BF16)

    ub = tm // FFN_UP_SPLIT
    rb = tm // FFN_ROW_SPLIT

    def up_proj(ch, r):
        bufs = (ubuf0, ubuf1)[ch % 2]
        r0 = r * ub
        for half in range(2):
            col = half * D_FF + ch * FF_CHUNK
            up = jnp.dot(hb[r0:r0 + ub, :], wu_ref[0, :, col:col + FF_CHUNK], preferred_element_type=F32)
            if r == 0:
                bufs[half, 0:FFN_HALO, :] = carry[:, col:col + FF_CHUNK]
            bufs[half, FFN_HALO + r0:FFN_HALO + r0 + ub, :] = up
            if r == FFN_UP_SPLIT - 1:
                carry[:, col:col + FF_CHUNK] = up[ub - FFN_HALO:ub, :]

    def conv(ch, half, r):
        bufs = (ubuf0, ubuf1)[ch % 2]
        col = half * D_FF + ch * FF_CHUNK
        out = cb_ref[:, col:col + FF_CHUNK]
        for kk in range(FFN_KERNEL):
            off = FFN_HALO - (FFN_KERNEL - 1) + kk + r * rb
            out = out + cw_ref[kk:kk + 1, col:col + FF_CHUNK] * bufs[half, off:off + rb, :]
        return out

    for r in range(FFN_UP_SPLIT):
        up_proj(0, r)
    for ch in range(n_chunks):
        col = ch * FF_CHUNK
        for r in range(FFN_ROW_SPLIT):
            if ch + 1 < n_chunks and r % (FFN_ROW_SPLIT // FFN_UP_SPLIT) == 0:
                up_proj(ch + 1, r // (FFN_ROW_SPLIT // FFN_UP_SPLIT))
            act = (_gelu(conv(ch, 0, r)) * conv(ch, 1, r)).astype(BF16)
            contrib = jnp.dot(act, wd_ref[0, col:col + FF_CHUNK, :], preferred_element_type=F32)
            rows = slice(r * rb, (r + 1) * rb)
            if ch == 0:
                yacc[rows, :] = contrib
            else:
                yacc[rows, :] += contrib

    xo_ref[...] = x1 + _rms(yacc[...], gqf_ref[...])


def _out_ffn(layer, x, a, b, c, wo_bf, gpm, gpf, gqf, wu_bf, cw, cb, wd_bf):
    seq = x.shape[0]
    tm = ROW_TILE
    rows = lambda n: pl.BlockSpec((tm, n), lambda i: (i, 0))

    def resident(shape):
        return pl.BlockSpec(shape, lambda i: (0,) * len(shape), pipeline_mode=pl.Buffered(1))

    def resident_of_layer(shape):
        return pl.BlockSpec((1,) + shape, lambda i: (layer,) + (0,) * len(shape),
                            pipeline_mode=pl.Buffered(1))

    return pl.pallas_call(
        _out_ffn_kernel,
        out_shape=jax.ShapeDtypeStruct((seq, D_MODEL), F32),
        grid=(seq // tm,),
        in_specs=[
            rows(D_MODEL), rows(GM_WIDTH), rows(DA_WIDTH), rows(CV_WIDTH),
            resident_of_layer((D_MODEL, D_MODEL)),
            resident((1, D_MODEL)), resident((1, D_MODEL)), resident((1, D_MODEL)),
            resident_of_layer((D_MODEL, 2 * D_FF)),
            resident((V7X_SUBLANES, 2 * D_FF)),
            resident((1, 2 * D_FF)),
            resident_of_layer((D_FF, D_MODEL)),
        ],
        out_specs=rows(D_MODEL),
        scratch_shapes=[
            pltpu.VMEM((FFN_HALO, 2 * D_FF), F32),
            pltpu.VMEM((2, FFN_HALO + tm, FF_CHUNK), F32),
            pltpu.VMEM((2, FFN_HALO + tm, FF_CHUNK), F32),
            pltpu.VMEM((tm, D_MODEL), F32),
        ],
        compiler_params=pltpu.CompilerParams(
            dimension_semantics=("arbitrary",),
            vmem_limit_bytes=V7X_SCOPED_VMEM_BYTES),
        name="out_ffn",
    )(x, a, b, c, wo_bf, gpm, gpf, gqf, wu_bf, cw, cb, wd_bf)


def _bucket_table():
    n = np.arange(MAX_DISTANCE + 1)
    max_exact = N_BUCKETS // 2
    nf = np.maximum(n, 1).astype(np.float32)
    large = max_exact + (np.log(nf / max_exact) / math.log(MAX_DISTANCE / max_exact)
                         * (N_BUCKETS - max_exact)).astype(np.int32)
    large = np.minimum(large, N_BUCKETS - 1)
    return np.where(n < max_exact, n, large).astype(np.int32)


def _group_mean_matrix():
    g = np.arange(CV_WIDTH) // (CV_WIDTH // CV_GROUPS)
    return jnp.asarray((g[:, None] == g[None, :]) / (CV_WIDTH // CV_GROUPS), dtype=BF16)


def _pad_rows(w, rows):
    return jnp.concatenate([w, jnp.zeros((rows - w.shape[0], w.shape[1]), w.dtype)], axis=0)


def kernel(x, w_in, w_out, gm_ln_g, gm_ln_b, gm_w_s, gm_b_s, da_lq1, da_lk1, da_lq2, da_lk2, da_subln_g, rel_bias, cv_dw_w, cv_dw_b, cv_ln_g, cv_ln_b, ffn_w_up, ffn_conv_w, ffn_conv_b, ffn_w_down, pre_mix_g, post_mix_g, pre_ffn_g, post_ffn_g):
    batch, seq, d_model = x.shape
    depth = w_in.shape[0]
    assert batch == 1 and d_model == D_MODEL and w_in.shape[2] == IN_WIDTH
    assert seq % ROW_TILE == 0 and ROW_TILE == ATT_TILE and ATT_TILE >= MAX_DISTANCE
    assert ATT_Q_SLABS == 2 and seq % (ATT_Q_SLABS * ATT_TILE) == 0
    assert ROW_TILE % CHUNK == 0 and D_FF % FF_CHUNK == 0 and CV_HALO >= CV_KERNEL - 1

    gmat = _group_mean_matrix()
    row = lambda p: p.reshape(1, -1)

    w_in_bf = w_in.astype(BF16)
    q0, v0 = 2 * GM_WIDTH, 2 * GM_WIDTH + 2 * DA_WIDTH
    wqt_bf = jnp.transpose(w_in_bf[:, :, q0:q0 + DA_WIDTH], (0, 2, 1))
    wvt_bf = jnp.transpose(w_in_bf[:, :, v0:v0 + DA_WIDTH], (0, 2, 1))
    w_out_bf = w_out.astype(BF16)
    w_up_bf = ffn_w_up.astype(BF16)
    w_down_bf = ffn_w_down.astype(BF16)

    xs = x[0]
    for l in range(depth):
        lambda_init = 0.8 - 0.6 * math.exp(-0.3 * l)
        bs_full = jnp.repeat(jnp.transpose(gm_b_s[l]), GM_HEAD_DIM, axis=1)
        a, q, k, v, c = _mix_in(
            l, xs, row(pre_mix_g[l]), w_in_bf, wqt_bf, wvt_bf,
            row(gm_ln_g[l]), row(gm_ln_b[l]), gm_w_s[l], bs_full,
            _pad_rows(cv_dw_w[l], CV_HALO), row(cv_dw_b[l]), row(cv_ln_g[l]), row(cv_ln_b[l]), gmat)
        b = _diff_attn(rel_bias, q, k, v, row(da_lq1[l]), row(da_lk1[l]), row(da_lq2[l]), row(da_lk2[l]),
                       row(da_subln_g[l]), lambda_init)
        xs = _out_ffn(
            l, xs, a, b, c, w_out_bf,
            row(post_mix_g[l]), row(pre_ffn_g[l]), row(post_ffn_g[l]),
            w_up_bf, _pad_rows(ffn_conv_w[l], V7X_SUBLANES), row(ffn_conv_b[l]),
            w_down_bf)
    return xs[None]
```

```python
import functools
import math

import numpy as np
import jax
import jax.numpy as jnp
from jax import lax
from jax.experimental import pallas as pl
from jax.experimental.pallas import tpu as pltpu

F32 = jnp.float32
BF16 = jnp.bfloat16

D_MODEL = 1024
GM_HEADS = 4
GM_WIDTH = 256
GM_HEAD_DIM = GM_WIDTH // GM_HEADS
CHUNK = 128
DA_HEADS = 4
DA_WIDTH = 512
DA_V_DIM = 128
DA_QK_DIM = 64
CV_GROUPS = 4
CV_WIDTH = 256
CV_KERNEL = 31
D_FF = 2816
FFN_KERNEL = 3
N_BUCKETS = 32
MAX_DISTANCE = 128
EPS = 1e-6
IN_WIDTH = 2 * GM_WIDTH + 3 * DA_WIDTH + 2 * CV_WIDTH

V7X_SUBLANES = 8
V7X_LANES = 128
V7X_MXU_DIM = 256
V7X_SCOPED_VMEM_BYTES = 60000 * 1024

ROW_TILE = 512
ATT_TILE = 512
ATT_Q_SLABS = 2
ATT_GROUP = V7X_MXU_DIM
CV_HALO = 32
FFN_HALO = V7X_SUBLANES
FF_CHUNK = V7X_MXU_DIM
FFN_UP_SPLIT = 2
FFN_ROW_SPLIT = 2

LOG2E = math.log2(math.e)


_GELU_K1 = -2.0 * math.sqrt(2.0 / math.pi) * LOG2E
_GELU_K3 = _GELU_K1 * 0.044715


def _gelu(x):
    return x / (1.0 + jnp.exp2(x * (_GELU_K1 + _GELU_K3 * (x * x))))


def _rms(x, g):
    return x * lax.rsqrt(jnp.mean(x * x, axis=-1, keepdims=True) + EPS) * g


def _split_dot(x, g_ref):
    hi = x.astype(BF16)
    lo = (x - hi.astype(F32)).astype(BF16)
    g = g_ref[...]
    return (jnp.dot(hi, g, preferred_element_type=F32)
            + jnp.dot(lo, g, preferred_element_type=F32))


def _mix_in_kernel(x_ref, g_ref, w_ref, wqt_ref, wvt_ref, lng_ref, lnb_ref, ws_ref, bs_ref,
                   cw_ref, cb_ref, cg_ref, cbeta_ref, gmat_ref,
                   a_ref, qt_ref, k_ref, vt_ref, c_ref, hbuf):
    tm = x_ref.shape[0]
    i = pl.program_id(0)

    x = x_ref[...]
    hb = _rms(x, g_ref[...]).astype(BF16)

    def proj(c0, n):
        return jnp.dot(hb, w_ref[0, :, c0:c0 + n], preferred_element_type=F32)

    def proj_t(wt_ref):
        return lax.dot_general(wt_ref[0], hb, (((1,), (1,)), ((), ())), preferred_element_type=F32)

    c0 = 2 * GM_WIDTH

    cv = proj(c0 + 3 * DA_WIDTH, 2 * CV_WIDTH)
    hg = cv[:, :CV_WIDTH] * jax.nn.sigmoid(cv[:, CV_WIDTH:])

    @pl.when(i == 0)
    def _():
        hbuf[0, 0:CV_HALO, :] = jnp.zeros((CV_HALO, CV_WIDTH), F32)

    hbuf[0, CV_HALO:CV_HALO + tm, :] = hg
    n_shift_rows = tm + CV_HALO - V7X_SUBLANES
    for b in range(1, V7X_SUBLANES):
        hbuf[b, 0:n_shift_rows, :] = hbuf[0, b:b + n_shift_rows, :]

    gm = _gelu(proj(0, 2 * GM_WIDTH))
    qt = (proj_t(wqt_ref) * (DA_QK_DIM ** -0.5 * LOG2E)).astype(BF16)
    qt_ref[:, 0, :, :] = qt.reshape(DA_HEADS, DA_V_DIM, tm)
    u = gm[:, :GM_WIDTH]
    v = gm[:, GM_WIDTH:]
    mu = jnp.mean(v, axis=-1, keepdims=True)
    d = v - mu
    var = jnp.mean(d * d, axis=-1, keepdims=True)
    vn = d * lax.rsqrt(var + EPS) * lng_ref[...] + lnb_ref[...]

    row = lax.broadcasted_iota(jnp.int32, (CHUNK, CHUNK), 0)
    col = lax.broadcasted_iota(jnp.int32, (CHUNK, CHUNK), 1)
    causal = row >= col
    wcat = jnp.concatenate(
        [jnp.where(causal, ws_ref[h], 0.0).astype(BF16) for h in range(GM_HEADS)], axis=1)
    head_of_lane = lax.broadcasted_iota(jnp.int32, (CHUNK, GM_WIDTH), 1) // GM_HEAD_DIM
    bs = bs_ref[...]
    for c in range(tm // CHUNK):
        r0 = c * CHUNK
        vc = vn[r0:r0 + CHUNK, :]
        vstack = jnp.concatenate(
            [jnp.where(head_of_lane == h, vc, 0.0).astype(BF16) for h in range(GM_HEADS)], axis=0)
        mixed = jnp.dot(wcat, vstack, preferred_element_type=F32) + bs
        a_ref[r0:r0 + CHUNK, :] = (u[r0:r0 + CHUNK, :] * mixed).astype(BF16)

    k_ref[...] = proj(c0 + DA_WIDTH, DA_WIDTH).astype(BF16)
    vt = proj_t(wvt_ref).astype(BF16).reshape(DA_HEADS, DA_V_DIM, tm)
    for kc in range(tm // V7X_LANES):
        vt_ref[:, 0, kc, :, :] = vt[:, :, kc * V7X_LANES:(kc + 1) * V7X_LANES]

    acc = jnp.broadcast_to(cb_ref[...], (tm, CV_WIDTH))
    for kk in range(CV_KERNEL):
        off = CV_HALO - (CV_KERNEL - 1) + kk
        b, a0 = off % V7X_SUBLANES, off - off % V7X_SUBLANES
        acc = acc + cw_ref[kk:kk + 1, :] * hbuf[b, a0:a0 + tm, :]
    hbuf[0, 0:CV_HALO, :] = hbuf[0, tm:tm + CV_HALO, :]

    gmu = _split_dot(acc, gmat_ref)
    dd = acc - gmu
    gvar = _split_dot(dd * dd, gmat_ref)
    y = dd * lax.rsqrt(gvar + EPS) * cg_ref[...] + cbeta_ref[...]
    c_ref[...] = (y * jax.nn.sigmoid(y)).astype(BF16)


def _mix_in(layer, x, g, w_bf, wqt_bf, wvt_bf, lng, lnb, ws, bs_full, cw, cb, cg, cbeta, gmat):
    seq = x.shape[0]
    tm = ROW_TILE
    full = lambda shape: pl.BlockSpec(shape, lambda i: (0,) * len(shape))
    of_layer = lambda shape: pl.BlockSpec((1,) + shape, lambda i: (layer,) + (0,) * len(shape))
    rows = lambda n: pl.BlockSpec((tm, n), lambda i: (i, 0))
    slabs = pl.BlockSpec((DA_HEADS, 1, DA_V_DIM, tm), lambda i: (0, i, 0, 0))
    slab_shape = jax.ShapeDtypeStruct((DA_HEADS, seq // tm, DA_V_DIM, tm), BF16)
    pieces = pl.BlockSpec((DA_HEADS, 1, tm // V7X_LANES, DA_V_DIM, V7X_LANES), lambda i: (0, i, 0, 0, 0))
    pieces_shape = jax.ShapeDtypeStruct((DA_HEADS, seq // tm, tm // V7X_LANES, DA_V_DIM, V7X_LANES), BF16)
    out_shapes = (
        jax.ShapeDtypeStruct((seq, GM_WIDTH), BF16),
        slab_shape,
        jax.ShapeDtypeStruct((seq, DA_WIDTH), BF16),
        pieces_shape,
        jax.ShapeDtypeStruct((seq, CV_WIDTH), BF16),
    )
    return pl.pallas_call(
        _mix_in_kernel,
        out_shape=out_shapes,
        grid=(seq // tm,),
        in_specs=[
            rows(D_MODEL),
            full((1, D_MODEL)),
            of_layer((D_MODEL, IN_WIDTH)),
            of_layer((DA_WIDTH, D_MODEL)), of_layer((DA_WIDTH, D_MODEL)),
            full((1, GM_WIDTH)), full((1, GM_WIDTH)),
            full((GM_HEADS, CHUNK, CHUNK)),
            full((CHUNK, GM_WIDTH)),
            full((CV_HALO, CV_WIDTH)),
            full((1, CV_WIDTH)), full((1, CV_WIDTH)), full((1, CV_WIDTH)),
            full((CV_WIDTH, CV_WIDTH)),
        ],
        out_specs=(rows(GM_WIDTH), slabs, rows(DA_WIDTH), pieces, rows(CV_WIDTH)),
        scratch_shapes=[pltpu.VMEM((V7X_SUBLANES, CV_HALO + tm, CV_WIDTH), F32)],
        compiler_params=pltpu.CompilerParams(
            dimension_semantics=("arbitrary",),
            vmem_limit_bytes=V7X_SCOPED_VMEM_BYTES),
        name="mix_in",
    )(x, g, w_bf, wqt_bf, wvt_bf, lng, lnb, ws, bs_full, cw, cb, cg, cbeta, gmat)


def _fill_bias(rb_ref, bias_s, head, t):
    sub = V7X_LANES
    table = _bucket_table()
    first_rel = [int(np.argmax(table == b)) for b in range(N_BUCKETS)]
    last = rb_ref[N_BUCKETS - 1, head]
    vals = [(rb_ref[b, head] - last) * LOG2E for b in range(N_BUCKETS)]
    d0 = (lax.broadcasted_iota(jnp.int32, (sub, sub), 1)
          - lax.broadcasted_iota(jnp.int32, (sub, sub), 0))

    def band(base):
        rel = d0 + base
        v = jnp.full((sub, sub), vals[0], F32)
        for b in range(1, N_BUCKETS):
            v = jnp.where(rel >= first_rel[b], vals[b], v)
        return jnp.where(rel < 0, -jnp.inf, v)

    bands = {0: band(0), sub: band(sub)}
    for cb in range(bias_s.shape[0] // sub):
        for rb in range(bias_s.shape[1] // sub):
            base = sub * (rb - cb) + t
            if base in bands:
                tile = bands[base]
            elif base > sub:
                tile = jnp.zeros((sub, sub), F32)
            else:
                tile = jnp.full((sub, sub), -jnp.inf, F32)
            bias_s[cb * sub:(cb + 1) * sub, rb * sub:(rb + 1) * sub] = tile


def _diff_attn_kernel(rb_ref, qt_ref, k_ref, vt_ref, lq1_ref, lk1_ref, lq2_ref, lk2_ref, sg_ref,
                      o_ref, qst, acc, m_s, l_s, s_buf, mx_buf, bias_s, *, lambda_init):
    t = ATT_TILE
    tq = o_ref.shape[0]
    i = pl.program_id(1)

    @pl.when(i == 0)
    def _():
        _fill_bias(rb_ref, bias_s, pl.program_id(0), t)

    n_groups = 2 * tq // ATT_GROUP
    cols = [slice(g * ATT_GROUP, (g + 1) * ATT_GROUP) for g in range(n_groups)]
    groups_per_slab = t // ATT_GROUP
    for slab in range(ATT_Q_SLABS):
        qt = qt_ref[0, slab]
        chan = lax.broadcasted_iota(jnp.int32, qt.shape, 0)
        zero = jnp.zeros_like(qt)
        q1 = jnp.where(chan < DA_QK_DIM, qt, zero)
        q2 = jnp.where(chan >= DA_QK_DIM, qt, zero)
        for gg in range(groups_per_slab):
            g = slab * groups_per_slab + gg
            qst[g] = q1[:, gg * ATT_GROUP:(gg + 1) * ATT_GROUP]
            qst[n_groups // 2 + g] = q2[:, gg * ATT_GROUP:(gg + 1) * ATT_GROUP]
    m_s[...] = jnp.full(m_s.shape, -jnp.inf, F32)
    l_s[...] = jnp.zeros(l_s.shape, F32)
    acc[...] = jnp.zeros(acc.shape, F32)

    def group_kind(g, d):
        r0 = (g * ATT_GROUP) % tq
        rel_min = r0 + t * (1 - d) - (t - 1)
        rel_max = r0 + ATT_GROUP - 1 + t * (1 - d)
        if rel_max < 0:
            return "masked"
        return "plain" if rel_min >= MAX_DISTANCE else "biased"

    def visible_keys(g, d):
        if d is None:
            return t
        r0 = (g * ATT_GROUP) % tq
        n = min(t, r0 + ATT_GROUP + t * (1 - d))
        return -(-n // V7X_MXU_DIM) * V7X_MXU_DIM

    def scores(j, slot, d=None):
        kb = k_ref[pl.ds(pl.multiple_of(j * t, t), t), :]
        for g in range(n_groups):
            if d is not None and group_kind(g, d) == "masked":
                continue
            n = visible_keys(g, d)
            s = jnp.dot(kb[0:n, :], qst[g], preferred_element_type=F32)
            s_buf[slot, g, 0:n, :] = s
            mx_buf[slot, :, cols[g]] = jnp.max(s, axis=0, keepdims=True)

    def softmax_pv(j, slot, d=None):
        vt = jnp.concatenate([vt_ref[0, j, kc] for kc in range(t // V7X_LANES)], axis=1)
        for g in range(n_groups):
            kind = "plain" if d is None else group_kind(g, d)
            if kind == "masked":
                continue
            n = visible_keys(g, d)
            s = s_buf[slot, g, 0:n, :]
            if kind == "plain":
                s_max = mx_buf[slot, :, cols[g]]
            else:
                r0 = (g * ATT_GROUP) % tq
                s = s + bias_s[d * t:d * t + n, r0:r0 + ATT_GROUP]
                s_max = jnp.max(s, axis=0, keepdims=True)
            m_prev = m_s[:, cols[g]]
            m_new = jnp.maximum(m_prev, s_max)
            alpha = jnp.exp2(m_prev - m_new)
            p = jnp.exp2(s - m_new)
            l_s[:, cols[g]] = alpha * l_s[:, cols[g]] + jnp.sum(p, axis=0, keepdims=True)
            acc[g] = alpha * acc[g] + jnp.dot(vt[:, 0:n], p.astype(BF16), preferred_element_type=F32)
            m_s[:, cols[g]] = m_new

    first = ATT_Q_SLABS * i
    scores(0, 0)

    def pair(j, d_second, d_next):
        scores(j + 1, 1)
        softmax_pv(j, 0)
        scores(j + 2, 0, d_next)
        softmax_pv(j + 1, 1, d_second)

    def far_pair(pp, carry):
        pair(2 * pp, None, None)
        return carry

    lax.fori_loop(0, i - 1, far_pair, 0)

    @pl.when(i > 0)
    def _():
        pair(first - 2, 0, 1)

    scores(first + 1, 1, 2)
    softmax_pv(first, 0, 1)
    softmax_pv(first + 1, 1, 2)

    half = n_groups // 2
    o1 = jnp.concatenate([acc[g] for g in range(half)], axis=1) / l_s[:, 0:tq]
    o2 = jnp.concatenate([acc[half + g] for g in range(half)], axis=1) / l_s[:, tq:2 * tq]
    lam = (jnp.exp(jnp.sum(lq1_ref[...] * lk1_ref[...], axis=-1, keepdims=True))
           - jnp.exp(jnp.sum(lq2_ref[...] * lk2_ref[...], axis=-1, keepdims=True))
           + lambda_init)
    ot = o1 - lam * o2
    ot = ot * lax.rsqrt(jnp.mean(ot * ot, axis=0, keepdims=True) + EPS)
    o_ref[...] = (jnp.transpose(ot) * sg_ref[...] * (1.0 - lambda_init)).astype(BF16)


def _diff_attn(rel_bias, qt, k, vt, lq1, lk1, lq2, lk2, sg, lambda_init):
    seq = k.shape[0]
    t = ATT_TILE
    tq = ATT_Q_SLABS * t
    vec = lambda n: pl.BlockSpec((1, n), lambda h, i: (0, 0))
    return pl.pallas_call(
        functools.partial(_diff_attn_kernel, lambda_init=lambda_init),
        out_shape=jax.ShapeDtypeStruct((seq, DA_WIDTH), BF16),
        grid=(DA_HEADS, seq // tq),
        in_specs=[
            pl.BlockSpec(memory_space=pltpu.SMEM),
            pl.BlockSpec((1, ATT_Q_SLABS, DA_V_DIM, t), lambda h, i: (h, i, 0, 0)),
            pl.BlockSpec((seq, DA_V_DIM), lambda h, i: (0, h)),
            pl.BlockSpec((1, seq // t, t // V7X_LANES, DA_V_DIM, V7X_LANES), lambda h, i: (h, 0, 0, 0, 0)),
            vec(DA_QK_DIM), vec(DA_QK_DIM), vec(DA_QK_DIM), vec(DA_QK_DIM),
            vec(DA_V_DIM),
        ],
        out_specs=pl.BlockSpec((tq, DA_V_DIM), lambda h, i: (i, h)),
        scratch_shapes=[
            pltpu.VMEM((2 * tq // ATT_GROUP, 2 * DA_QK_DIM, ATT_GROUP), BF16),
            pltpu.VMEM((2 * tq // ATT_GROUP, DA_V_DIM, ATT_GROUP), F32),
            pltpu.VMEM((1, 2 * tq), F32),
            pltpu.VMEM((1, 2 * tq), F32),
            pltpu.VMEM((2, 2 * tq // ATT_GROUP, t, ATT_GROUP), F32),
            pltpu.VMEM((2, 1, 2 * tq), F32),
            pltpu.VMEM(((ATT_Q_SLABS + 1) * t, tq), F32),
        ],
        compiler_params=pltpu.CompilerParams(
            dimension_semantics=("arbitrary", "arbitrary"),
            vmem_limit_bytes=V7X_SCOPED_VMEM_BYTES),
        name="diff_attn",
    )(rel_bias, qt, k, vt, lq1, lk1, lq2, lk2, sg)


def _out_ffn_kernel(x_ref, a_ref, b_ref, c_ref, wo_ref, gpm_ref, gpf_ref, gqf_ref,
                    wu_ref, cw_ref, cb_ref, wd_ref, xo_ref, carry, ubuf0, ubuf1, yacc):
    tm = x_ref.shape[0]
    i = pl.program_id(0)
    n_chunks = D_FF // FF_CHUNK

    @pl.when(i == 0)
    def _():
        carry[...] = jnp.zeros(carry.shape, F32)

    o_a, o_b, o_c = 0, GM_WIDTH, GM_WIDTH + DA_WIDTH
    mix = (jnp.dot(a_ref[...], wo_ref[0, o_a:o_b, :], preferred_element_type=F32)
           + jnp.dot(b_ref[...], wo_ref[0, o_b:o_c, :], preferred_element_type=F32)
           + jnp.dot(c_ref[...], wo_ref[0, o_c:, :], preferred_element_type=F32))
    x1 = x_ref[...] + _rms(mix, gpm_ref[...])
    hb = _rms(x1, gpf_ref[...]).astype(BF16)

    ub = tm // FFN_UP_SPLIT
    rb = tm // FFN_ROW_SPLIT

    def up_proj(ch, r):
        bufs = (ubuf0, ubuf1)[ch % 2]
        r0 = r * ub
        for half in range(2):
            col = half * D_FF + ch * FF_CHUNK
            up = jnp.dot(hb[r0:r0 + ub, :], wu_ref[0, :, col:col + FF_CHUNK], preferred_element_type=F32)
            if r == 0:
                bufs[half, 0:FFN_HALO, :] = carry[:, col:col + FF_CHUNK]
            bufs[half, FFN_HALO + r0:FFN_HALO + r0 + ub, :] = up
            if r == FFN_UP_SPLIT - 1:
                carry[:, col:col + FF_CHUNK] = up[ub - FFN_HALO:ub, :]

    def conv(ch, half, r):
        bufs = (ubuf0, ubuf1)[ch % 2]
        col = half * D_FF + ch * FF_CHUNK
        out = cb_ref[:, col:col + FF_CHUNK]
        for kk in range(FFN_KERNEL):
            off = FFN_HALO - (FFN_KERNEL - 1) + kk + r * rb
            out = out + cw_ref[kk:kk + 1, col:col + FF_CHUNK] * bufs[half, off:off + rb, :]
        return out

    for r in range(FFN_UP_SPLIT):
        up_proj(0, r)
    for ch in range(n_chunks):
        col = ch * FF_CHUNK
        for r in range(FFN_ROW_SPLIT):
            if ch + 1 < n_chunks and r % (FFN_ROW_SPLIT // FFN_UP_SPLIT) == 0:
                up_proj(ch + 1, r // (FFN_ROW_SPLIT // FFN_UP_SPLIT))
            act = (_gelu(conv(ch, 0, r)) * conv(ch, 1, r)).astype(BF16)
            contrib = jnp.dot(act, wd_ref[0, col:col + FF_CHUNK, :], preferred_element_type=F32)
            rows = slice(r * rb, (r + 1) * rb)
            if ch == 0:
                yacc[rows, :] = contrib
            else:
                yacc[rows, :] += contrib

    xo_ref[...] = x1 + _rms(yacc[...], gqf_ref[...])


def _out_ffn(layer, x, a, b, c, wo_bf, gpm, gpf, gqf, wu_bf, cw, cb, wd_bf):
    seq = x.shape[0]
    tm = ROW_TILE
    rows = lambda n: pl.BlockSpec((tm, n), lambda i: (i, 0))

    def resident(shape):
        return pl.BlockSpec(shape, lambda i: (0,) * len(shape), pipeline_mode=pl.Buffered(1))

    def resident_of_layer(shape):
        return pl.BlockSpec((1,) + shape, lambda i: (layer,) + (0,) * len(shape),
                            pipeline_mode=pl.Buffered(1))

    return pl.pallas_call(
        _out_ffn_kernel,
        out_shape=jax.ShapeDtypeStruct((seq, D_MODEL), F32),
        grid=(seq // tm,),
        in_specs=[
            rows(D_MODEL), rows(GM_WIDTH), rows(DA_WIDTH), rows(CV_WIDTH),
            resident_of_layer((D_MODEL, D_MODEL)),
            resident((1, D_MODEL)), resident((1, D_MODEL)), resident((1, D_MODEL)),
            resident_of_layer((D_MODEL, 2 * D_FF)),
            resident((V7X_SUBLANES, 2 * D_FF)),
            resident((1, 2 * D_FF)),
            resident_of_layer((D_FF, D_MODEL)),
        ],
        out_specs=rows(D_MODEL),
        scratch_shapes=[
            pltpu.VMEM((FFN_HALO, 2 * D_FF), F32),
            pltpu.VMEM((2, FFN_HALO + tm, FF_CHUNK), F32),
            pltpu.VMEM((2, FFN_HALO + tm, FF_CHUNK), F32),
            pltpu.VMEM((tm, D_MODEL), F32),
        ],
        compiler_params=pltpu.CompilerParams(
            dimension_semantics=("arbitrary",),
            vmem_limit_bytes=V7X_SCOPED_VMEM_BYTES),
        name="out_ffn",
    )(x, a, b, c, wo_bf, gpm, gpf, gqf, wu_bf, cw, cb, wd_bf)


def _bucket_table():
    n = np.arange(MAX_DISTANCE + 1)
    max_exact = N_BUCKETS // 2
    nf = np.maximum(n, 1).astype(np.float32)
    large = max_exact + (np.log(nf / max_exact) / math.log(MAX_DISTANCE / max_exact)
                         * (N_BUCKETS - max_exact)).astype(np.int32)
    large = np.minimum(large, N_BUCKETS - 1)
    return np.where(n < max_exact, n, large).astype(np.int32)


def _group_mean_matrix():
    g = np.arange(CV_WIDTH) // (CV_WIDTH // CV_GROUPS)
    return jnp.asarray((g[:, None] == g[None, :]) / (CV_WIDTH // CV_GROUPS), dtype=BF16)


def _pad_rows(w, rows):
    return jnp.concatenate([w, jnp.zeros((rows - w.shape[0], w.shape[1]), w.dtype)], axis=0)


def kernel(x, w_in, w_out, gm_ln_g, gm_ln_b, gm_w_s, gm_b_s, da_lq1, da_lk1, da_lq2, da_lk2, da_subln_g, rel_bias, cv_dw_w, cv_dw_b, cv_ln_g, cv_ln_b, ffn_w_up, ffn_conv_w, ffn_conv_b, ffn_w_down, pre_mix_g, post_mix_g, pre_ffn_g, post_ffn_g):
    batch, seq, d_model = x.shape
    depth = w_in.shape[0]
    assert batch == 1 and d_model == D_MODEL and w_in.shape[2] == IN_WIDTH
    assert seq % ROW_TILE == 0 and ROW_TILE == ATT_TILE and ATT_TILE >= MAX_DISTANCE
    assert ATT_Q_SLABS == 2 and seq % (ATT_Q_SLABS * ATT_TILE) == 0
    assert ROW_TILE % CHUNK == 0 and D_FF % FF_CHUNK == 0 and CV_HALO >= CV_KERNEL - 1

    gmat = _group_mean_matrix()
    row = lambda p: p.reshape(1, -1)

    w_in_bf = w_in.astype(BF16)
    q0, v0 = 2 * GM_WIDTH, 2 * GM_WIDTH + 2 * DA_WIDTH
    wqt_bf = jnp.transpose(w_in_bf[:, :, q0:q0 + DA_WIDTH], (0, 2, 1))
    wvt_bf = jnp.transpose(w_in_bf[:, :, v0:v0 + DA_WIDTH], (0, 2, 1))
    w_out_bf = w_out.astype(BF16)
    w_up_bf = ffn_w_up.astype(BF16)
    w_down_bf = ffn_w_down.astype(BF16)

    xs = x[0]
    for l in range(depth):
        lambda_init = 0.8 - 0.6 * math.exp(-0.3 * l)
        bs_full = jnp.repeat(jnp.transpose(gm_b_s[l]), GM_HEAD_DIM, axis=1)
        a, q, k, v, c = _mix_in(
            l, xs, row(pre_mix_g[l]), w_in_bf, wqt_bf, wvt_bf,
            row(gm_ln_g[l]), row(gm_ln_b[l]), gm_w_s[l], bs_full,
            _pad_rows(cv_dw_w[l], CV_HALO), row(cv_dw_b[l]), row(cv_ln_g[l]), row(cv_ln_b[l]), gmat)
        b = _diff_attn(rel_bias, q, k, v, row(da_lq1[l]), row(da_lk1[l]), row(da_lq2[l]), row(da_lk2[l]),
                       row(da_subln_g[l]), lambda_init)
        xs = _out_ffn(
            l, xs, a, b, c, w_out_bf,
            row(post_mix_g[l]), row(pre_ffn_g[l]), row(post_ffn_g[l]),
            w_up_bf, _pad_rows(ffn_conv_w[l], V7X_SUBLANES), row(ffn_conv_b[l]),
            w_down_bf)
    return xs[None]
```

```python
import functools
import math

import numpy as np
import jax
import jax.numpy as jnp
from jax import lax
from jax.experimental import pallas as pl
from jax.experimental.pallas import tpu as pltpu

F32 = jnp.float32
BF16 = jnp.bfloat16

D_MODEL = 1024
GM_HEADS = 4
GM_WIDTH = 256
GM_HEAD_DIM = GM_WIDTH // GM_HEADS
CHUNK = 128
DA_HEADS = 4
DA_WIDTH = 512
DA_V_DIM = 128
DA_QK_DIM = 64
CV_GROUPS = 4
CV_WIDTH = 256
CV_KERNEL = 31
D_FF = 2816
FFN_KERNEL = 3
N_BUCKETS = 32
MAX_DISTANCE = 128
EPS = 1e-6
IN_WIDTH = 2 * GM_WIDTH + 3 * DA_WIDTH + 2 * CV_WIDTH

V7X_SUBLANES = 8
V7X_LANES = 128
V7X_MXU_DIM = 256
V7X_SCOPED_VMEM_BYTES = 60000 * 1024

ROW_TILE = 512
ATT_TILE = 512
ATT_Q_SLABS = 2
ATT_GROUP = V7X_MXU_DIM
CV_HALO = 32
FFN_HALO = V7X_SUBLANES
FF_CHUNK = V7X_MXU_DIM
FFN_UP_SPLIT = 1
FFN_ROW_SPLIT = 1

LOG2E = math.log2(math.e)


_GELU_K1 = -2.0 * math.sqrt(2.0 / math.pi) * LOG2E
_GELU_K3 = _GELU_K1 * 0.044715


def _gelu(x):
    return x / (1.0 + jnp.exp2(x * (_GELU_K1 + _GELU_K3 * (x * x))))


def _rms(x, g):
    return x * lax.rsqrt(jnp.mean(x * x, axis=-1, keepdims=True) + EPS) * g


def _split_dot(x, g_ref):
    hi = x.astype(BF16)
    lo = (x - hi.astype(F32)).astype(BF16)
    g = g_ref[...]
    return (jnp.dot(hi, g, preferred_element_type=F32)
            + jnp.dot(lo, g, preferred_element_type=F32))


def _mix_in_kernel(x_ref, g_ref, w_ref, wqt_ref, wvt_ref, lng_ref, lnb_ref, ws_ref, bs_ref,
                   cw_ref, cb_ref, cg_ref, cbeta_ref, gmat_ref,
                   a_ref, qt_ref, k_ref, vt_ref, c_ref, hbuf):
    tm = x_ref.shape[0]
    i = pl.program_id(0)

    x = x_ref[...]
    hb = _rms(x, g_ref[...]).astype(BF16)

    def proj(c0, n):
        return jnp.dot(hb, w_ref[0, :, c0:c0 + n], preferred_element_type=F32)

    def proj_t(wt_ref):
        return lax.dot_general(wt_ref[0], hb, (((1,), (1,)), ((), ())), preferred_element_type=F32)

    c0 = 2 * GM_WIDTH

    cv = proj(c0 + 3 * DA_WIDTH, 2 * CV_WIDTH)
    hg = cv[:, :CV_WIDTH] * jax.nn.sigmoid(cv[:, CV_WIDTH:])

    @pl.when(i == 0)
    def _():
        hbuf[0, 0:CV_HALO, :] = jnp.zeros((CV_HALO, CV_WIDTH), F32)

    hbuf[0, CV_HALO:CV_HALO + tm, :] = hg
    n_shift_rows = tm + CV_HALO - V7X_SUBLANES
    for b in range(1, V7X_SUBLANES):
        hbuf[b, 0:n_shift_rows, :] = hbuf[0, b:b + n_shift_rows, :]

    gm = _gelu(proj(0, 2 * GM_WIDTH))
    qt = (proj_t(wqt_ref) * (DA_QK_DIM ** -0.5 * LOG2E)).astype(BF16)
    qt_ref[:, 0, :, :] = qt.reshape(DA_HEADS, DA_V_DIM, tm)
    u = gm[:, :GM_WIDTH]
    v = gm[:, GM_WIDTH:]
    mu = jnp.mean(v, axis=-1, keepdims=True)
    d = v - mu
    var = jnp.mean(d * d, axis=-1, keepdims=True)
    vn = d * lax.rsqrt(var + EPS) * lng_ref[...] + lnb_ref[...]

    row = lax.broadcasted_iota(jnp.int32, (CHUNK, CHUNK), 0)
    col = lax.broadcasted_iota(jnp.int32, (CHUNK, CHUNK), 1)
    causal = row >= col
    wcat = jnp.concatenate(
        [jnp.where(causal, ws_ref[h], 0.0).astype(BF16) for h in range(GM_HEADS)], axis=1)
    head_of_lane = lax.broadcasted_iota(jnp.int32, (CHUNK, GM_WIDTH), 1) // GM_HEAD_DIM
    bs = bs_ref[...]
    for c in range(tm // CHUNK):
        r0 = c * CHUNK
        vc = vn[r0:r0 + CHUNK, :]
        vstack = jnp.concatenate(
            [jnp.where(head_of_lane == h, vc, 0.0).astype(BF16) for h in range(GM_HEADS)], axis=0)
        mixed = jnp.dot(wcat, vstack, preferred_element_type=F32) + bs
        a_ref[r0:r0 + CHUNK, :] = (u[r0:r0 + CHUNK, :] * mixed).astype(BF16)

    k_ref[...] = proj(c0 + DA_WIDTH, DA_WIDTH).astype(BF16)
    vt = proj_t(wvt_ref).astype(BF16).reshape(DA_HEADS, DA_V_DIM, tm)
    for kc in range(tm // V7X_LANES):
        vt_ref[:, 0, kc, :, :] = vt[:, :, kc * V7X_LANES:(kc + 1) * V7X_LANES]

    acc = jnp.broadcast_to(cb_ref[...], (tm, CV_WIDTH))
    for kk in range(CV_KERNEL):
        off = CV_HALO - (CV_KERNEL - 1) + kk
        b, a0 = off % V7X_SUBLANES, off - off % V7X_SUBLANES
        acc = acc + cw_ref[kk:kk + 1, :] * hbuf[b, a0:a0 + tm, :]
    hbuf[0, 0:CV_HALO, :] = hbuf[0, tm:tm + CV_HALO, :]

    gmu = _split_dot(acc, gmat_ref)
    dd = acc - gmu
    gvar = _split_dot(dd * dd, gmat_ref)
    y = dd * lax.rsqrt(gvar + EPS) * cg_ref[...] + cbeta_ref[...]
    c_ref[...] = (y * jax.nn.sigmoid(y)).astype(BF16)


def _mix_in(layer, x, g, w_bf, wqt_bf, wvt_bf, lng, lnb, ws, bs_full, cw, cb, cg, cbeta, gmat):
    seq = x.shape[0]
    tm = ROW_TILE
    full = lambda shape: pl.BlockSpec(shape, lambda i: (0,) * len(shape))
    of_layer = lambda shape: pl.BlockSpec((1,) + shape, lambda i: (layer,) + (0,) * len(shape))
    rows = lambda n: pl.BlockSpec((tm, n), lambda i: (i, 0))
    slabs = pl.BlockSpec((DA_HEADS, 1, DA_V_DIM, tm), lambda i: (0, i, 0, 0))
    slab_shape = jax.ShapeDtypeStruct((DA_HEADS, seq // tm, DA_V_DIM, tm), BF16)
    pieces = pl.BlockSpec((DA_HEADS, 1, tm // V7X_LANES, DA_V_DIM, V7X_LANES), lambda i: (0, i, 0, 0, 0))
    pieces_shape = jax.ShapeDtypeStruct((DA_HEADS, seq // tm, tm // V7X_LANES, DA_V_DIM, V7X_LANES), BF16)
    out_shapes = (
        jax.ShapeDtypeStruct((seq, GM_WIDTH), BF16),
        slab_shape,
        jax.ShapeDtypeStruct((seq, DA_WIDTH), BF16),
        pieces_shape,
        jax.ShapeDtypeStruct((seq, CV_WIDTH), BF16),
    )
    return pl.pallas_call(
        _mix_in_kernel,
        out_shape=out_shapes,
        grid=(seq // tm,),
        in_specs=[
            rows(D_MODEL),
            full((1, D_MODEL)),
            of_layer((D_MODEL, IN_WIDTH)),
            of_layer((DA_WIDTH, D_MODEL)), of_layer((DA_WIDTH, D_MODEL)),
            full((1, GM_WIDTH)), full((1, GM_WIDTH)),
            full((GM_HEADS, CHUNK, CHUNK)),
            full((CHUNK, GM_WIDTH)),
            full((CV_HALO, CV_WIDTH)),
            full((1, CV_WIDTH)), full((1, CV_WIDTH)), full((1, CV_WIDTH)),
            full((CV_WIDTH, CV_WIDTH)),
        ],
        out_specs=(rows(GM_WIDTH), slabs, rows(DA_WIDTH), pieces, rows(CV_WIDTH)),
        scratch_shapes=[pltpu.VMEM((V7X_SUBLANES, CV_HALO + tm, CV_WIDTH), F32)],
        compiler_params=pltpu.CompilerParams(
            dimension_semantics=("arbitrary",),
            vmem_limit_bytes=V7X_SCOPED_VMEM_BYTES),
        name="mix_in",
    )(x, g, w_bf, wqt_bf, wvt_bf, lng, lnb, ws, bs_full, cw, cb, cg, cbeta, gmat)


def _fill_bias(rb_ref, bias_s, head, t):
    sub = V7X_LANES
    table = _bucket_table()
    first_rel = [int(np.argmax(table == b)) for b in range(N_BUCKETS)]
    last = rb_ref[N_BUCKETS - 1, head]
    vals = [(rb_ref[b, head] - last) * LOG2E for b in range(N_BUCKETS)]
    d0 = (lax.broadcasted_iota(jnp.int32, (sub, sub), 1)
          - lax.broadcasted_iota(jnp.int32, (sub, sub), 0))

    def band(base):
        rel = d0 + base
        v = jnp.full((sub, sub), vals[0], F32)
        for b in range(1, N_BUCKETS):
            v = jnp.where(rel >= first_rel[b], vals[b], v)
        return jnp.where(rel < 0, -jnp.inf, v)

    bands = {0: band(0), sub: band(sub)}
    for cb in range(bias_s.shape[0] // sub):
        for rb in range(bias_s.shape[1] // sub):
            base = sub * (rb - cb) + t
            if base in bands:
                tile = bands[base]
            elif base > sub:
                tile = jnp.zeros((sub, sub), F32)
            else:
                tile = jnp.full((sub, sub), -jnp.inf, F32)
            bias_s[cb * sub:(cb + 1) * sub, rb * sub:(rb + 1) * sub] = tile


def _diff_attn_kernel(rb_ref, qt_ref, k_ref, vt_ref, lq1_ref, lk1_ref, lq2_ref, lk2_ref, sg_ref,
                      o_ref, qst, acc, m_s, l_s, s_buf, mx_buf, bias_s, *, lambda_init):
    t = ATT_TILE
    tq = o_ref.shape[0]
    i = pl.program_id(1)

    @pl.when(i == 0)
    def _():
        _fill_bias(rb_ref, bias_s, pl.program_id(0), t)

    n_groups = 2 * tq // ATT_GROUP
    cols = [slice(g * ATT_GROUP, (g + 1) * ATT_GROUP) for g in range(n_groups)]
    groups_per_slab = t // ATT_GROUP
    for slab in range(ATT_Q_SLABS):
        qt = qt_ref[0, slab]
        chan = lax.broadcasted_iota(jnp.int32, qt.shape, 0)
        zero = jnp.zeros_like(qt)
        q1 = jnp.where(chan < DA_QK_DIM, qt, zero)
        q2 = jnp.where(chan >= DA_QK_DIM, qt, zero)
        for gg in range(groups_per_slab):
            g = slab * groups_per_slab + gg
            qst[g] = q1[:, gg * ATT_GROUP:(gg + 1) * ATT_GROUP]
            qst[n_groups // 2 + g] = q2[:, gg * ATT_GROUP:(gg + 1) * ATT_GROUP]
    m_s[...] = jnp.full(m_s.shape, -jnp.inf, F32)
    l_s[...] = jnp.zeros(l_s.shape, F32)
    acc[...] = jnp.zeros(acc.shape, F32)

    def group_kind(g, d):
        r0 = (g * ATT_GROUP) % tq
        rel_min = r0 + t * (1 - d) - (t - 1)
        rel_max = r0 + ATT_GROUP - 1 + t * (1 - d)
        if rel_max < 0:
            return "masked"
        return "plain" if rel_min >= MAX_DISTANCE else "biased"

    def visible_keys(g, d):
        if d is None:
            return t
        r0 = (g * ATT_GROUP) % tq
        n = min(t, r0 + ATT_GROUP + t * (1 - d))
        return -(-n // V7X_MXU_DIM) * V7X_MXU_DIM

    def scores(j, slot, d=None):
        kb = k_ref[pl.ds(pl.multiple_of(j * t, t), t), :]
        for g in range(n_groups):
            if d is not None and group_kind(g, d) == "masked":
                continue
            n = visible_keys(g, d)
            s = jnp.dot(kb[0:n, :], qst[g], preferred_element_type=F32)
            s_buf[slot, g, 0:n, :] = s
            mx_buf[slot, :, cols[g]] = jnp.max(s, axis=0, keepdims=True)

    def softmax_pv(j, slot, d=None):
        vt = jnp.concatenate([vt_ref[0, j, kc] for kc in range(t // V7X_LANES)], axis=1)
        for g in range(n_groups):
            kind = "plain" if d is None else group_kind(g, d)
            if kind == "masked":
                continue
            n = visible_keys(g, d)
            s = s_buf[slot, g, 0:n, :]
            if kind == "plain":
                s_max = mx_buf[slot, :, cols[g]]
            else:
                r0 = (g * ATT_GROUP) % tq
                s = s + bias_s[d * t:d * t + n, r0:r0 + ATT_GROUP]
                s_max = jnp.max(s, axis=0, keepdims=True)
            m_prev = m_s[:, cols[g]]
            m_new = jnp.maximum(m_prev, s_max)
            alpha = jnp.exp2(m_prev - m_new)
            p = jnp.exp2(s - m_new)
            l_s[:, cols[g]] = alpha * l_s[:, cols[g]] + jnp.sum(p, axis=0, keepdims=True)
            acc[g] = alpha * acc[g] + jnp.dot(vt[:, 0:n], p.astype(BF16), preferred_element_type=F32)
            m_s[:, cols[g]] = m_new

    first = ATT_Q_SLABS * i
    scores(0, 0)

    def pair(j, d_second, d_next):
        scores(j + 1, 1)
        softmax_pv(j, 0)
        scores(j + 2, 0, d_next)
        softmax_pv(j + 1, 1, d_second)

    def far_pair(pp, carry):
        pair(2 * pp, None, None)
        return carry

    lax.fori_loop(0, i - 1, far_pair, 0)

    @pl.when(i > 0)
    def _():
        pair(first - 2, 0, 1)

    scores(first + 1, 1, 2)
    softmax_pv(first, 0, 1)
    softmax_pv(first + 1, 1, 2)

    half = n_groups // 2
    inv_l = 1.0 / l_s[...]
    o1 = jnp.concatenate([acc[g] for g in range(half)], axis=1) * inv_l[:, 0:tq]
    o2 = jnp.concatenate([acc[half + g] for g in range(half)], axis=1) * inv_l[:, tq:2 * tq]
    lam = (jnp.exp(jnp.sum(lq1_ref[...] * lk1_ref[...], axis=-1, keepdims=True))
           - jnp.exp(jnp.sum(lq2_ref[...] * lk2_ref[...], axis=-1, keepdims=True))
           + lambda_init)
    ot = o1 - lam * o2
    ot = ot * lax.rsqrt(jnp.mean(ot * ot, axis=0, keepdims=True) + EPS)
    o_ref[...] = (jnp.transpose(ot) * sg_ref[...] * (1.0 - lambda_init)).astype(BF16)


def _diff_attn(rel_bias, qt, k, vt, lq1, lk1, lq2, lk2, sg, lambda_init):
    seq = k.shape[0]
    t = ATT_TILE
    tq = ATT_Q_SLABS * t
    vec = lambda n: pl.BlockSpec((1, n), lambda h, i: (0, 0))
    return pl.pallas_call(
        functools.partial(_diff_attn_kernel, lambda_init=lambda_init),
        out_shape=jax.ShapeDtypeStruct((seq, DA_WIDTH), BF16),
        grid=(DA_HEADS, seq // tq),
        in_specs=[
            pl.BlockSpec(memory_space=pltpu.SMEM),
            pl.BlockSpec((1, ATT_Q_SLABS, DA_V_DIM, t), lambda h, i: (h, i, 0, 0)),
            pl.BlockSpec((seq, DA_V_DIM), lambda h, i: (0, h)),
            pl.BlockSpec((1, seq // t, t // V7X_LANES, DA_V_DIM, V7X_LANES), lambda h, i: (h, 0, 0, 0, 0)),
            vec(DA_QK_DIM), vec(DA_QK_DIM), vec(DA_QK_DIM), vec(DA_QK_DIM),
            vec(DA_V_DIM),
        ],
        out_specs=pl.BlockSpec((tq, DA_V_DIM), lambda h, i: (i, h)),
        scratch_shapes=[
            pltpu.VMEM((2 * tq // ATT_GROUP, 2 * DA_QK_DIM, ATT_GROUP), BF16),
            pltpu.VMEM((2 * tq // ATT_GROUP, DA_V_DIM, ATT_GROUP), F32),
            pltpu.VMEM((1, 2 * tq), F32),
            pltpu.VMEM((1, 2 * tq), F32),
            pltpu.VMEM((2, 2 * tq // ATT_GROUP, t, ATT_GROUP), F32),
            pltpu.VMEM((2, 1, 2 * tq), F32),
            pltpu.VMEM(((ATT_Q_SLABS + 1) * t, tq), F32),
        ],
        compiler_params=pltpu.CompilerParams(
            dimension_semantics=("arbitrary", "arbitrary"),
            vmem_limit_bytes=V7X_SCOPED_VMEM_BYTES),
        name="diff_attn",
    )(rel_bias, qt, k, vt, lq1, lk1, lq2, lk2, sg)


def _out_ffn_kernel(x_ref, a_ref, b_ref, c_ref, wo_ref, gpm_ref, gpf_ref, gqf_ref,
                    wu_ref, cw_ref, cb_ref, wd_ref, xo_ref, carry, ubuf0, ubuf1, yacc):
    tm = x_ref.shape[0]
    i = pl.program_id(0)
    n_chunks = D_FF // FF_CHUNK

    @pl.when(i == 0)
    def _():
        carry[...] = jnp.zeros(carry.shape, F32)

    o_a, o_b, o_c = 0, GM_WIDTH, GM_WIDTH + DA_WIDTH
    mix = (jnp.dot(a_ref[...], wo_ref[0, o_a:o_b, :], preferred_element_type=F32)
           + jnp.dot(b_ref[...], wo_ref[0, o_b:o_c, :], preferred_element_type=F32)
           + jnp.dot(c_ref[...], wo_ref[0, o_c:, :], preferred_element_type=F32))
    x1 = x_ref[...] + _rms(mix, gpm_ref[...])
    hb = _rms(x1, gpf_ref[...]).astype(BF16)

    ub = tm // FFN_UP_SPLIT
    rb = tm // FFN_ROW_SPLIT

    def up_proj(ch, r):
        bufs = (ubuf0, ubuf1)[ch % 2]
        r0 = r * ub
        for half in range(2):
            col = half * D_FF + ch * FF_CHUNK
            up = jnp.dot(hb[r0:r0 + ub, :], wu_ref[0, :, col:col + FF_CHUNK], preferred_element_type=F32)
            if r == 0:
                bufs[half, 0:FFN_HALO, :] = carry[:, col:col + FF_CHUNK]
            bufs[half, FFN_HALO + r0:FFN_HALO + r0 + ub, :] = up
            if r == FFN_UP_SPLIT - 1:
                carry[:, col:col + FF_CHUNK] = up[ub - FFN_HALO:ub, :]

    def conv(ch, half, r):
        bufs = (ubuf0, ubuf1)[ch % 2]
        col = half * D_FF + ch * FF_CHUNK
        out = cb_ref[:, col:col + FF_CHUNK]
        for kk in range(FFN_KERNEL):
            off = FFN_HALO - (FFN_KERNEL - 1) + kk + r * rb
            out = out + cw_ref[kk:kk + 1, col:col + FF_CHUNK] * bufs[half, off:off + rb, :]
        return out

    for r in range(FFN_UP_SPLIT):
        up_proj(0, r)
    for ch in range(n_chunks):
        col = ch * FF_CHUNK
        for r in range(FFN_ROW_SPLIT):
            if ch + 1 < n_chunks and r % (FFN_ROW_SPLIT // FFN_UP_SPLIT) == 0:
                up_proj(ch + 1, r // (FFN_ROW_SPLIT // FFN_UP_SPLIT))
            act = (_gelu(conv(ch, 0, r)) * conv(ch, 1, r)).astype(BF16)
            contrib = jnp.dot(act, wd_ref[0, col:col + FF_CHUNK, :], preferred_element_type=F32)
            rows = slice(r * rb, (r + 1) * rb)
            if ch == 0:
                yacc[rows, :] = contrib
            else:
                yacc[rows, :] += contrib

    xo_ref[...] = x1 + _rms(yacc[...], gqf_ref[...])


def _out_ffn(layer, x, a, b, c, wo_bf, gpm, gpf, gqf, wu_bf, cw, cb, wd_bf):
    seq = x.shape[0]
    tm = ROW_TILE
    rows = lambda n: pl.BlockSpec((tm, n), lambda i: (i, 0))

    def resident(shape):
        return pl.BlockSpec(shape, lambda i: (0,) * len(shape), pipeline_mode=pl.Buffered(1))

    def resident_of_layer(shape):
        return pl.BlockSpec((1,) + shape, lambda i: (layer,) + (0,) * len(shape),
                            pipeline_mode=pl.Buffered(1))

    return pl.pallas_call(
        _out_ffn_kernel,
        out_shape=jax.ShapeDtypeStruct((seq, D_MODEL), F32),
        grid=(seq // tm,),
        in_specs=[
            rows(D_MODEL), rows(GM_WIDTH), rows(DA_WIDTH), rows(CV_WIDTH),
            resident_of_layer((D_MODEL, D_MODEL)),
            resident((1, D_MODEL)), resident((1, D_MODEL)), resident((1, D_MODEL)),
            resident_of_layer((D_MODEL, 2 * D_FF)),
            resident((V7X_SUBLANES, 2 * D_FF)),
            resident((1, 2 * D_FF)),
            resident_of_layer((D_FF, D_MODEL)),
        ],
        out_specs=rows(D_MODEL),
        scratch_shapes=[
            pltpu.VMEM((FFN_HALO, 2 * D_FF), F32),
            pltpu.VMEM((2, FFN_HALO + tm, FF_CHUNK), F32),
            pltpu.VMEM((2, FFN_HALO + tm, FF_CHUNK), F32),
            pltpu.VMEM((tm, D_MODEL), F32),
        ],
        compiler_params=pltpu.CompilerParams(
            dimension_semantics=("arbitrary",),
            vmem_limit_bytes=V7X_SCOPED_VMEM_BYTES),
        name="out_ffn",
    )(x, a, b, c, wo_bf, gpm, gpf, gqf, wu_bf, cw, cb, wd_bf)


def _bucket_table():
    n = np.arange(MAX_DISTANCE + 1)
    max_exact = N_BUCKETS // 2
    nf = np.maximum(n, 1).astype(np.float32)
    large = max_exact + (np.log(nf / max_exact) / math.log(MAX_DISTANCE / max_exact)
                         * (N_BUCKETS - max_exact)).astype(np.int32)
    large = np.minimum(large, N_BUCKETS - 1)
    return np.where(n < max_exact, n, large).astype(np.int32)


def _group_mean_matrix():
    g = np.arange(CV_WIDTH) // (CV_WIDTH // CV_GROUPS)
    return jnp.asarray((g[:, None] == g[None, :]) / (CV_WIDTH // CV_GROUPS), dtype=BF16)


def _pad_rows(w, rows):
    return jnp.concatenate([w, jnp.zeros((rows - w.shape[0], w.shape[1]), w.dtype)], axis=0)


def kernel(x, w_in, w_out, gm_ln_g, gm_ln_b, gm_w_s, gm_b_s, da_lq1, da_lk1, da_lq2, da_lk2, da_subln_g, rel_bias, cv_dw_w, cv_dw_b, cv_ln_g, cv_ln_b, ffn_w_up, ffn_conv_w, ffn_conv_b, ffn_w_down, pre_mix_g, post_mix_g, pre_ffn_g, post_ffn_g):
    batch, seq, d_model = x.shape
    depth = w_in.shape[0]
    assert batch == 1 and d_model == D_MODEL and w_in.shape[2] == IN_WIDTH
    assert seq % ROW_TILE == 0 and ROW_TILE == ATT_TILE and ATT_TILE >= MAX_DISTANCE
    assert ATT_Q_SLABS == 2 and seq % (ATT_Q_SLABS * ATT_TILE) == 0
    assert ROW_TILE % CHUNK == 0 and D_FF % FF_CHUNK == 0 and CV_HALO >= CV_KERNEL - 1

    gmat = _group_mean_matrix()
    row = lambda p: p.reshape(1, -1)

    w_in_bf = w_in.astype(BF16)
    q0, v0 = 2 * GM_WIDTH, 2 * GM_WIDTH + 2 * DA_WIDTH
    wqt_bf = jnp.transpose(w_in_bf[:, :, q0:q0 + DA_WIDTH], (0, 2, 1))
    wvt_bf = jnp.transpose(w_in_bf[:, :, v0:v0 + DA_WIDTH], (0, 2, 1))
    w_out_bf = w_out.astype(BF16)
    w_up_bf = ffn_w_up.astype(BF16)
    w_down_bf = ffn_w_down.astype(BF16)

    xs = x[0]
    for l in range(depth):
        lambda_init = 0.8 - 0.6 * math.exp(-0.3 * l)
        bs_full = jnp.repeat(jnp.transpose(gm_b_s[l]), GM_HEAD_DIM, axis=1)
        a, q, k, v, c = _mix_in(
            l, xs, row(pre_mix_g[l]), w_in_bf, wqt_bf, wvt_bf,
            row(gm_ln_g[l]), row(gm_ln_b[l]), gm_w_s[l], bs_full,
            _pad_rows(cv_dw_w[l], CV_HALO), row(cv_dw_b[l]), row(cv_ln_g[l]), row(cv_ln_b[l]), gmat)
        b = _diff_attn(rel_bias, q, k, v, row(da_lq1[l]), row(da_lk1[l]), row(da_lq2[l]), row(da_lk2[l]),
                       row(da_subln_g[l]), lambda_init)
        xs = _out_ffn(
            l, xs, a, b, c, w_out_bf,
            row(post_mix_g[l]), row(pre_ffn_g[l]), row(post_ffn_g[l]),
            w_up_bf, _pad_rows(ffn_conv_w[l], V7X_SUBLANES), row(ffn_conv_b[l]),
            w_down_bf)
    return xs[None]
```

```python
import functools
import math

import numpy as np
import jax
import jax.numpy as jnp
from jax import lax
from jax.experimental import pallas as pl
from jax.experimental.pallas import tpu as pltpu

F32 = jnp.float32
BF16 = jnp.bfloat16

D_MODEL = 1024
GM_HEADS = 4
GM_WIDTH = 256
GM_HEAD_DIM = GM_WIDTH // GM_HEADS
CHUNK = 128
DA_HEADS = 4
DA_WIDTH = 512
DA_V_DIM = 128
DA_QK_DIM = 64
CV_GROUPS = 4
CV_WIDTH = 256
CV_KERNEL = 31
D_FF = 2816
FFN_KERNEL = 3
N_BUCKETS = 32
MAX_DISTANCE = 128
EPS = 1e-6
IN_WIDTH = 2 * GM_WIDTH + 3 * DA_WIDTH + 2 * CV_WIDTH

V7X_SUBLANES = 8
V7X_LANES = 128
V7X_MXU_DIM = 256
V7X_SCOPED_VMEM_BYTES = 60000 * 1024

ROW_TILE = 512
ATT_TILE = 512
ATT_Q_SLABS = 2
ATT_GROUP = V7X_MXU_DIM
CV_HALO = 32
FFN_HALO = V7X_SUBLANES
FF_CHUNK = V7X_MXU_DIM
FFN_UP_SPLIT = 1
FFN_ROW_SPLIT = 1

LOG2E = math.log2(math.e)


_GELU_K1 = -2.0 * math.sqrt(2.0 / math.pi) * LOG2E
_GELU_K3 = _GELU_K1 * 0.044715


def _gelu(x):
    return x / (1.0 + jnp.exp2(x * (_GELU_K1 + _GELU_K3 * (x * x))))


def _rms(x, g):
    return x * lax.rsqrt(jnp.mean(x * x, axis=-1, keepdims=True) + EPS) * g


def _split_dot(x, g_ref):
    hi = x.astype(BF16)
    lo = (x - hi.astype(F32)).astype(BF16)
    g = g_ref[...]
    return (jnp.dot(hi, g, preferred_element_type=F32)
            + jnp.dot(lo, g, preferred_element_type=F32))


def _mix_in_kernel(x_ref, g_ref, w_ref, wqvt_ref, lng_ref, lnb_ref, ws_ref, bs_ref,
                   cw_ref, cb_ref, cg_ref, cbeta_ref, gmat_ref,
                   a_ref, qt_ref, k_ref, vt_ref, c_ref, hbuf):
    tm = x_ref.shape[0]
    i = pl.program_id(0)

    x = x_ref[...]
    hb = _rms(x, g_ref[...]).astype(BF16)

    def proj(c0, n):
        return jnp.dot(hb, w_ref[0, :, c0:c0 + n], preferred_element_type=F32)

    def proj_t(wt_ref):
        return lax.dot_general(wt_ref[0], hb, (((1,), (1,)), ((), ())), preferred_element_type=F32)

    c0 = 2 * GM_WIDTH

    cv = proj(c0 + 3 * DA_WIDTH, 2 * CV_WIDTH)
    hg = cv[:, :CV_WIDTH] * jax.nn.sigmoid(cv[:, CV_WIDTH:])

    @pl.when(i == 0)
    def _():
        hbuf[0, 0:CV_HALO, :] = jnp.zeros((CV_HALO, CV_WIDTH), F32)

    hbuf[0, CV_HALO:CV_HALO + tm, :] = hg
    n_shift_rows = tm + CV_HALO - V7X_SUBLANES
    for b in range(1, V7X_SUBLANES):
        hbuf[b, 0:n_shift_rows, :] = hbuf[0, b:b + n_shift_rows, :]

    gm = _gelu(proj(0, 2 * GM_WIDTH))
    qvt = proj_t(wqvt_ref)
    qt = (qvt[0:DA_WIDTH, :] * (DA_QK_DIM ** -0.5 * LOG2E)).astype(BF16)
    qt_ref[:, 0, :, :] = qt.reshape(DA_HEADS, DA_V_DIM, tm)
    u = gm[:, :GM_WIDTH]
    v = gm[:, GM_WIDTH:]
    mu = jnp.mean(v, axis=-1, keepdims=True)
    d = v - mu
    var = jnp.mean(d * d, axis=-1, keepdims=True)
    vn = d * lax.rsqrt(var + EPS) * lng_ref[...] + lnb_ref[...]

    row = lax.broadcasted_iota(jnp.int32, (CHUNK, CHUNK), 0)
    col = lax.broadcasted_iota(jnp.int32, (CHUNK, CHUNK), 1)
    causal = row >= col
    wcat = jnp.concatenate(
        [jnp.where(causal, ws_ref[h], 0.0).astype(BF16) for h in range(GM_HEADS)], axis=1)
    head_of_lane = lax.broadcasted_iota(jnp.int32, (CHUNK, GM_WIDTH), 1) // GM_HEAD_DIM
    bs = bs_ref[...]
    for c in range(tm // CHUNK):
        r0 = c * CHUNK
        vc = vn[r0:r0 + CHUNK, :]
        vstack = jnp.concatenate(
            [jnp.where(head_of_lane == h, vc, 0.0).astype(BF16) for h in range(GM_HEADS)], axis=0)
        mixed = jnp.dot(wcat, vstack, preferred_element_type=F32) + bs
        a_ref[r0:r0 + CHUNK, :] = (u[r0:r0 + CHUNK, :] * mixed).astype(BF16)

    k_ref[...] = proj(c0 + DA_WIDTH, DA_WIDTH).astype(BF16)
    vt = qvt[DA_WIDTH:, :].astype(BF16).reshape(DA_HEADS, DA_V_DIM, tm)
    for kc in range(tm // V7X_LANES):
        vt_ref[:, 0, kc, :, :] = vt[:, :, kc * V7X_LANES:(kc + 1) * V7X_LANES]

    acc = jnp.broadcast_to(cb_ref[...], (tm, CV_WIDTH))
    for kk in range(CV_KERNEL):
        off = CV_HALO - (CV_KERNEL - 1) + kk
        b, a0 = off % V7X_SUBLANES, off - off % V7X_SUBLANES
        acc = acc + cw_ref[kk:kk + 1, :] * hbuf[b, a0:a0 + tm, :]
    hbuf[0, 0:CV_HALO, :] = hbuf[0, tm:tm + CV_HALO, :]

    gmu = _split_dot(acc, gmat_ref)
    dd = acc - gmu
    gvar = _split_dot(dd * dd, gmat_ref)
    y = dd * lax.rsqrt(gvar + EPS) * cg_ref[...] + cbeta_ref[...]
    c_ref[...] = (y * jax.nn.sigmoid(y)).astype(BF16)


def _mix_in(layer, x, g, w_bf, wqvt_bf, lng, lnb, ws, bs_full, cw, cb, cg, cbeta, gmat):
    seq = x.shape[0]
    tm = ROW_TILE
    full = lambda shape: pl.BlockSpec(shape, lambda i: (0,) * len(shape))
    of_layer = lambda shape: pl.BlockSpec((1,) + shape, lambda i: (layer,) + (0,) * len(shape))
    rows = lambda n: pl.BlockSpec((tm, n), lambda i: (i, 0))
    slabs = pl.BlockSpec((DA_HEADS, 1, DA_V_DIM, tm), lambda i: (0, i, 0, 0))
    slab_shape = jax.ShapeDtypeStruct((DA_HEADS, seq // tm, DA_V_DIM, tm), BF16)
    pieces = pl.BlockSpec((DA_HEADS, 1, tm // V7X_LANES, DA_V_DIM, V7X_LANES), lambda i: (0, i, 0, 0, 0))
    pieces_shape = jax.ShapeDtypeStruct((DA_HEADS, seq // tm, tm // V7X_LANES, DA_V_DIM, V7X_LANES), BF16)
    out_shapes = (
        jax.ShapeDtypeStruct((seq, GM_WIDTH), BF16),
        slab_shape,
        jax.ShapeDtypeStruct((seq, DA_WIDTH), BF16),
        pieces_shape,
        jax.ShapeDtypeStruct((seq, CV_WIDTH), BF16),
    )
    return pl.pallas_call(
        _mix_in_kernel,
        out_shape=out_shapes,
        grid=(seq // tm,),
        in_specs=[
            rows(D_MODEL),
            full((1, D_MODEL)),
            of_layer((D_MODEL, IN_WIDTH)),
            of_layer((2 * DA_WIDTH, D_MODEL)),
            full((1, GM_WIDTH)), full((1, GM_WIDTH)),
            full((GM_HEADS, CHUNK, CHUNK)),
            full((CHUNK, GM_WIDTH)),
            full((CV_HALO, CV_WIDTH)),
            full((1, CV_WIDTH)), full((1, CV_WIDTH)), full((1, CV_WIDTH)),
            full((CV_WIDTH, CV_WIDTH)),
        ],
        out_specs=(rows(GM_WIDTH), slabs, rows(DA_WIDTH), pieces, rows(CV_WIDTH)),
        scratch_shapes=[pltpu.VMEM((V7X_SUBLANES, CV_HALO + tm, CV_WIDTH), F32)],
        compiler_params=pltpu.CompilerParams(
            dimension_semantics=("arbitrary",),
            vmem_limit_bytes=V7X_SCOPED_VMEM_BYTES),
        name="mix_in",
    )(x, g, w_bf, wqvt_bf, lng, lnb, ws, bs_full, cw, cb, cg, cbeta, gmat)


def _fill_bias(rb_ref, bias_s, head, t):
    sub = V7X_LANES
    table = _bucket_table()
    first_rel = [int(np.argmax(table == b)) for b in range(N_BUCKETS)]
    last = rb_ref[N_BUCKETS - 1, head]
    vals = [(rb_ref[b, head] - last) * LOG2E for b in range(N_BUCKETS)]
    d0 = (lax.broadcasted_iota(jnp.int32, (sub, sub), 1)
          - lax.broadcasted_iota(jnp.int32, (sub, sub), 0))

    def band(base):
        rel = d0 + base
        v = jnp.full((sub, sub), vals[0], F32)
        for b in range(1, N_BUCKETS):
            v = jnp.where(rel >= first_rel[b], vals[b], v)
        return jnp.where(rel < 0, -jnp.inf, v)

    bands = {0: band(0), sub: band(sub)}
    for cb in range(bias_s.shape[0] // sub):
        for rb in range(bias_s.shape[1] // sub):
            base = sub * (rb - cb) + t
            if base in bands:
                tile = bands[base]
            elif base > sub:
                tile = jnp.zeros((sub, sub), F32)
            else:
                tile = jnp.full((sub, sub), -jnp.inf, F32)
            bias_s[cb * sub:(cb + 1) * sub, rb * sub:(rb + 1) * sub] = tile


def _diff_attn_kernel(rb_ref, qt_ref, k_ref, vt_ref, lq1_ref, lk1_ref, lq2_ref, lk2_ref, sg_ref,
                      o_ref, qst, acc, m_s, l_s, s_buf, mx_buf, bias_s, *, lambda_init):
    t = ATT_TILE
    tq = o_ref.shape[0]
    i = pl.program_id(1)

    @pl.when(i == 0)
    def _():
        _fill_bias(rb_ref, bias_s, pl.program_id(0), t)

    n_groups = 2 * tq // ATT_GROUP
    cols = [slice(g * ATT_GROUP, (g + 1) * ATT_GROUP) for g in range(n_groups)]
    groups_per_slab = t // ATT_GROUP
    for slab in range(ATT_Q_SLABS):
        qt = qt_ref[0, slab]
        chan = lax.broadcasted_iota(jnp.int32, qt.shape, 0)
        zero = jnp.zeros_like(qt)
        q1 = jnp.where(chan < DA_QK_DIM, qt, zero)
        q2 = jnp.where(chan >= DA_QK_DIM, qt, zero)
        for gg in range(groups_per_slab):
            g = slab * groups_per_slab + gg
            qst[g] = q1[:, gg * ATT_GROUP:(gg + 1) * ATT_GROUP]
            qst[n_groups // 2 + g] = q2[:, gg * ATT_GROUP:(gg + 1) * ATT_GROUP]
    m_s[...] = jnp.full(m_s.shape, -jnp.inf, F32)
    l_s[...] = jnp.zeros(l_s.shape, F32)
    acc[...] = jnp.zeros(acc.shape, F32)

    def group_kind(g, d):
        r0 = (g * ATT_GROUP) % tq
        rel_min = r0 + t * (1 - d) - (t - 1)
        rel_max = r0 + ATT_GROUP - 1 + t * (1 - d)
        if rel_max < 0:
            return "masked"
        return "plain" if rel_min >= MAX_DISTANCE else "biased"

    def visible_keys(g, d):
        if d is None:
            return t
        r0 = (g * ATT_GROUP) % tq
        n = min(t, r0 + ATT_GROUP + t * (1 - d))
        return -(-n // V7X_MXU_DIM) * V7X_MXU_DIM

    def scores(j, slot, d=None):
        kb = k_ref[pl.ds(pl.multiple_of(j * t, t), t), :]
        for g in range(n_groups):
            if d is not None and group_kind(g, d) == "masked":
                continue
            n = visible_keys(g, d)
            s = jnp.dot(kb[0:n, :], qst[g], preferred_element_type=F32)
            s_buf[slot, g, 0:n, :] = s
            mx_buf[slot, :, cols[g]] = jnp.max(s, axis=0, keepdims=True)

    def softmax_pv(j, slot, d=None):
        vt = jnp.concatenate([vt_ref[0, j, kc] for kc in range(t // V7X_LANES)], axis=1)
        for g in range(n_groups):
            kind = "plain" if d is None else group_kind(g, d)
            if kind == "masked":
                continue
            n = visible_keys(g, d)
            s = s_buf[slot, g, 0:n, :]
            if kind == "plain":
                s_max = mx_buf[slot, :, cols[g]]
            else:
                r0 = (g * ATT_GROUP) % tq
                s = s + bias_s[d * t:d * t + n, r0:r0 + ATT_GROUP]
                s_max = jnp.max(s, axis=0, keepdims=True)
            m_prev = m_s[:, cols[g]]
            m_new = jnp.maximum(m_prev, s_max)
            alpha = jnp.exp2(m_prev - m_new)
            p = jnp.exp2(s - m_new)
            l_s[:, cols[g]] = alpha * l_s[:, cols[g]] + jnp.sum(p, axis=0, keepdims=True)
            acc[g] = alpha * acc[g] + jnp.dot(vt[:, 0:n], p.astype(BF16), preferred_element_type=F32)
            m_s[:, cols[g]] = m_new

    first = ATT_Q_SLABS * i
    scores(0, 0)

    def pair(j, d_second, d_next):
        scores(j + 1, 1)
        softmax_pv(j, 0)
        scores(j + 2, 0, d_next)
        softmax_pv(j + 1, 1, d_second)

    def far_pair(pp, carry):
        pair(2 * pp, None, None)
        return carry

    lax.fori_loop(0, i - 1, far_pair, 0)

    @pl.when(i > 0)
    def _():
        pair(first - 2, 0, 1)

    scores(first + 1, 1, 2)
    softmax_pv(first, 0, 1)
    softmax_pv(first + 1, 1, 2)

    half = n_groups // 2
    inv_l = 1.0 / l_s[...]
    o1 = jnp.concatenate([acc[g] for g in range(half)], axis=1) * inv_l[:, 0:tq]
    o2 = jnp.concatenate([acc[half + g] for g in range(half)], axis=1) * inv_l[:, tq:2 * tq]
    lam = (jnp.exp(jnp.sum(lq1_ref[...] * lk1_ref[...], axis=-1, keepdims=True))
           - jnp.exp(jnp.sum(lq2_ref[...] * lk2_ref[...], axis=-1, keepdims=True))
           + lambda_init)
    ot = o1 - lam * o2
    ot = ot * lax.rsqrt(jnp.mean(ot * ot, axis=0, keepdims=True) + EPS)
    o_ref[...] = (jnp.transpose(ot) * sg_ref[...] * (1.0 - lambda_init)).astype(BF16)


def _diff_attn(rel_bias, qt, k, vt, lq1, lk1, lq2, lk2, sg, lambda_init):
    seq = k.shape[0]
    t = ATT_TILE
    tq = ATT_Q_SLABS * t
    vec = lambda n: pl.BlockSpec((1, n), lambda h, i: (0, 0))
    return pl.pallas_call(
        functools.partial(_diff_attn_kernel, lambda_init=lambda_init),
        out_shape=jax.ShapeDtypeStruct((seq, DA_WIDTH), BF16),
        grid=(DA_HEADS, seq // tq),
        in_specs=[
            pl.BlockSpec(memory_space=pltpu.SMEM),
            pl.BlockSpec((1, ATT_Q_SLABS, DA_V_DIM, t), lambda h, i: (h, i, 0, 0)),
            pl.BlockSpec((seq, DA_V_DIM), lambda h, i: (0, h)),
            pl.BlockSpec((1, seq // t, t // V7X_LANES, DA_V_DIM, V7X_LANES), lambda h, i: (h, 0, 0, 0, 0)),
            vec(DA_QK_DIM), vec(DA_QK_DIM), vec(DA_QK_DIM), vec(DA_QK_DIM),
            vec(DA_V_DIM),
        ],
        out_specs=pl.BlockSpec((tq, DA_V_DIM), lambda h, i: (i, h)),
        scratch_shapes=[
            pltpu.VMEM((2 * tq // ATT_GROUP, 2 * DA_QK_DIM, ATT_GROUP), BF16),
            pltpu.VMEM((2 * tq // ATT_GROUP, DA_V_DIM, ATT_GROUP), F32),
            pltpu.VMEM((1, 2 * tq), F32),
            pltpu.VMEM((1, 2 * tq), F32),
            pltpu.VMEM((2, 2 * tq // ATT_GROUP, t, ATT_GROUP), F32),
            pltpu.VMEM((2, 1, 2 * tq), F32),
            pltpu.VMEM(((ATT_Q_SLABS + 1) * t, tq), F32),
        ],
        compiler_params=pltpu.CompilerParams(
            dimension_semantics=("arbitrary", "arbitrary"),
            vmem_limit_bytes=V7X_SCOPED_VMEM_BYTES),
        name="diff_attn",
    )(rel_bias, qt, k, vt, lq1, lk1, lq2, lk2, sg)


def _out_ffn_kernel(x_ref, a_ref, b_ref, c_ref, wo_ref, gpm_ref, gpf_ref, gqf_ref,
                    wu_ref, cw_ref, cb_ref, wd_ref, xo_ref, carry, ubuf0, ubuf1, yacc):
    tm = x_ref.shape[0]
    i = pl.program_id(0)
    n_chunks = D_FF // FF_CHUNK

    @pl.when(i == 0)
    def _():
        carry[...] = jnp.zeros(carry.shape, F32)

    o_a, o_b, o_c = 0, GM_WIDTH, GM_WIDTH + DA_WIDTH
    mix = (jnp.dot(a_ref[...], wo_ref[0, o_a:o_b, :], preferred_element_type=F32)
           + jnp.dot(b_ref[...], wo_ref[0, o_b:o_c, :], preferred_element_type=F32)
           + jnp.dot(c_ref[...], wo_ref[0, o_c:, :], preferred_element_type=F32))
    x1 = x_ref[...] + _rms(mix, gpm_ref[...])
    hb = _rms(x1, gpf_ref[...]).astype(BF16)

    ub = tm // FFN_UP_SPLIT
    rb = tm // FFN_ROW_SPLIT

    def up_proj(ch, r):
        bufs = (ubuf0, ubuf1)[ch % 2]
        r0 = r * ub
        for half in range(2):
            col = half * D_FF + ch * FF_CHUNK
            up = jnp.dot(hb[r0:r0 + ub, :], wu_ref[0, :, col:col + FF_CHUNK], preferred_element_type=F32)
            if r == 0:
                bufs[half, 0:FFN_HALO, :] = carry[:, col:col + FF_CHUNK]
            bufs[half, FFN_HALO + r0:FFN_HALO + r0 + ub, :] = up
            if r == FFN_UP_SPLIT - 1:
                carry[:, col:col + FF_CHUNK] = up[ub - FFN_HALO:ub, :]

    def conv(ch, half, r):
        bufs = (ubuf0, ubuf1)[ch % 2]
        col = half * D_FF + ch * FF_CHUNK
        out = cb_ref[:, col:col + FF_CHUNK]
        for kk in range(FFN_KERNEL):
            off = FFN_HALO - (FFN_KERNEL - 1) + kk + r * rb
            out = out + cw_ref[kk:kk + 1, col:col + FF_CHUNK] * bufs[half, off:off + rb, :]
        return out

    for r in range(FFN_UP_SPLIT):
        up_proj(0, r)
    for ch in range(n_chunks):
        col = ch * FF_CHUNK
        for r in range(FFN_ROW_SPLIT):
            if ch + 1 < n_chunks and r % (FFN_ROW_SPLIT // FFN_UP_SPLIT) == 0:
                up_proj(ch + 1, r // (FFN_ROW_SPLIT // FFN_UP_SPLIT))
            act = (_gelu(conv(ch, 0, r)) * conv(ch, 1, r)).astype(BF16)
            contrib = jnp.dot(act, wd_ref[0, col:col + FF_CHUNK, :], preferred_element_type=F32)
            rows = slice(r * rb, (r + 1) * rb)
            if ch == 0:
                yacc[rows, :] = contrib
            else:
                yacc[rows, :] += contrib

    xo_ref[...] = x1 + _rms(yacc[...], gqf_ref[...])


def _out_ffn(layer, x, a, b, c, wo_bf, gpm, gpf, gqf, wu_bf, cw, cb, wd_bf):
    seq = x.shape[0]
    tm = ROW_TILE
    rows = lambda n: pl.BlockSpec((tm, n), lambda i: (i, 0))

    def resident(shape):
        return pl.BlockSpec(shape, lambda i: (0,) * len(shape), pipeline_mode=pl.Buffered(1))

    def resident_of_layer(shape):
        return pl.BlockSpec((1,) + shape, lambda i: (layer,) + (0,) * len(shape),
                            pipeline_mode=pl.Buffered(1))

    return pl.pallas_call(
        _out_ffn_kernel,
        out_shape=jax.ShapeDtypeStruct((seq, D_MODEL), F32),
        grid=(seq // tm,),
        in_specs=[
            rows(D_MODEL), rows(GM_WIDTH), rows(DA_WIDTH), rows(CV_WIDTH),
            resident_of_layer((D_MODEL, D_MODEL)),
            resident((1, D_MODEL)), resident((1, D_MODEL)), resident((1, D_MODEL)),
            resident_of_layer((D_MODEL, 2 * D_FF)),
            resident((V7X_SUBLANES, 2 * D_FF)),
            resident((1, 2 * D_FF)),
            resident_of_layer((D_FF, D_MODEL)),
        ],
        out_specs=rows(D_MODEL),
        scratch_shapes=[
            pltpu.VMEM((FFN_HALO, 2 * D_FF), F32),
            pltpu.VMEM((2, FFN_HALO + tm, FF_CHUNK), F32),
            pltpu.VMEM((2, FFN_HALO + tm, FF_CHUNK), F32),
            pltpu.VMEM((tm, D_MODEL), F32),
        ],
        compiler_params=pltpu.CompilerParams(
            dimension_semantics=("arbitrary",),
            vmem_limit_bytes=V7X_SCOPED_VMEM_BYTES),
        name="out_ffn",
    )(x, a, b, c, wo_bf, gpm, gpf, gqf, wu_bf, cw, cb, wd_bf)


def _bucket_table():
    n = np.arange(MAX_DISTANCE + 1)
    max_exact = N_BUCKETS // 2
    nf = np.maximum(n, 1).astype(np.float32)
    large = max_exact + (np.log(nf / max_exact) / math.log(MAX_DISTANCE / max_exact)
                         * (N_BUCKETS - max_exact)).astype(np.int32)
    large = np.minimum(large, N_BUCKETS - 1)
    return np.where(n < max_exact, n, large).astype(np.int32)


def _group_mean_matrix():
    g = np.arange(CV_WIDTH) // (CV_WIDTH // CV_GROUPS)
    return jnp.asarray((g[:, None] == g[None, :]) / (CV_WIDTH // CV_GROUPS), dtype=BF16)


def _pad_rows(w, rows):
    return jnp.concatenate([w, jnp.zeros((rows - w.shape[0], w.shape[1]), w.dtype)], axis=0)


def kernel(x, w_in, w_out, gm_ln_g, gm_ln_b, gm_w_s, gm_b_s, da_lq1, da_lk1, da_lq2, da_lk2, da_subln_g, rel_bias, cv_dw_w, cv_dw_b, cv_ln_g, cv_ln_b, ffn_w_up, ffn_conv_w, ffn_conv_b, ffn_w_down, pre_mix_g, post_mix_g, pre_ffn_g, post_ffn_g):
    batch, seq, d_model = x.shape
    depth = w_in.shape[0]
    assert batch == 1 and d_model == D_MODEL and w_in.shape[2] == IN_WIDTH
    assert seq % ROW_TILE == 0 and ROW_TILE == ATT_TILE and ATT_TILE >= MAX_DISTANCE
    assert ATT_Q_SLABS == 2 and seq % (ATT_Q_SLABS * ATT_TILE) == 0
    assert ROW_TILE % CHUNK == 0 and D_FF % FF_CHUNK == 0 and CV_HALO >= CV_KERNEL - 1

    gmat = _group_mean_matrix()
    row = lambda p: p.reshape(1, -1)

    w_in_bf = w_in.astype(BF16)
    q0, v0 = 2 * GM_WIDTH, 2 * GM_WIDTH + 2 * DA_WIDTH
    wqvt_bf = jnp.transpose(
        jnp.concatenate([w_in_bf[:, :, q0:q0 + DA_WIDTH], w_in_bf[:, :, v0:v0 + DA_WIDTH]], axis=2), (0, 2, 1))
    w_out_bf = w_out.astype(BF16)
    w_up_bf = ffn_w_up.astype(BF16)
    w_down_bf = ffn_w_down.astype(BF16)

    xs = x[0]
    for l in range(depth):
        lambda_init = 0.8 - 0.6 * math.exp(-0.3 * l)
        bs_full = jnp.repeat(jnp.transpose(gm_b_s[l]), GM_HEAD_DIM, axis=1)
        a, q, k, v, c = _mix_in(
            l, xs, row(pre_mix_g[l]), w_in_bf, wqvt_bf,
            row(gm_ln_g[l]), row(gm_ln_b[l]), gm_w_s[l], bs_full,
            _pad_rows(cv_dw_w[l], CV_HALO), row(cv_dw_b[l]), row(cv_ln_g[l]), row(cv_ln_b[l]), gmat)
        b = _diff_attn(rel_bias, q, k, v, row(da_lq1[l]), row(da_lk1[l]), row(da_lq2[l]), row(da_lk2[l]),
                       row(da_subln_g[l]), lambda_init)
        xs = _out_ffn(
            l, xs, a, b, c, w_out_bf,
            row(post_mix_g[l]), row(pre_ffn_g[l]), row(post_ffn_g[l]),
            w_up_bf, _pad_rows(ffn_conv_w[l], V7X_SUBLANES), row(ffn_conv_b[l]),
            w_down_bf)
    return xs[None]
```

```python
import functools
import math

import numpy as np
import jax
import jax.numpy as jnp
from jax import lax
from jax.experimental import pallas as pl
from jax.experimental.pallas import tpu as pltpu

F32 = jnp.float32
BF16 = jnp.bfloat16

D_MODEL = 1024
GM_HEADS = 4
GM_WIDTH = 256
GM_HEAD_DIM = GM_WIDTH // GM_HEADS
CHUNK = 128
DA_HEADS = 4
DA_WIDTH = 512
DA_V_DIM = 128
DA_QK_DIM = 64
CV_GROUPS = 4
CV_WIDTH = 256
CV_KERNEL = 31
D_FF = 2816
FFN_KERNEL = 3
N_BUCKETS = 32
MAX_DISTANCE = 128
EPS = 1e-6
IN_WIDTH = 2 * GM_WIDTH + 3 * DA_WIDTH + 2 * CV_WIDTH

V7X_SUBLANES = 8
V7X_LANES = 128
V7X_MXU_DIM = 256
V7X_SCOPED_VMEM_BYTES = 60000 * 1024

ROW_TILE = 512
ATT_TILE = 512
ATT_Q_SLABS = 2
ATT_GROUP = V7X_MXU_DIM
CV_HALO = 32
FFN_HALO = V7X_SUBLANES
FF_CHUNK = V7X_MXU_DIM
FFN_UP_SPLIT = 1
FFN_ROW_SPLIT = 1

LOG2E = math.log2(math.e)


_GELU_K1 = -2.0 * math.sqrt(2.0 / math.pi) * LOG2E
_GELU_K3 = _GELU_K1 * 0.044715


def _gelu(x):
    return x / (1.0 + jnp.exp2(x * (_GELU_K1 + _GELU_K3 * (x * x))))


def _rms(x, g):
    return x * lax.rsqrt(jnp.mean(x * x, axis=-1, keepdims=True) + EPS) * g


def _split_dot(x, g_ref):
    hi = x.astype(BF16)
    lo = (x - hi.astype(F32)).astype(BF16)
    g = g_ref[...]
    return (jnp.dot(hi, g, preferred_element_type=F32)
            + jnp.dot(lo, g, preferred_element_type=F32))


def _mix_in_kernel(x_ref, g_ref, w_ref, wqvt_ref, lng_ref, lnb_ref, ws_ref, bs_ref,
                   cw_ref, cb_ref, cg_ref, cbeta_ref, gmat_ref,
                   a_ref, qt_ref, k_ref, vt_ref, c_ref, hbuf):
    tm = x_ref.shape[0]
    i = pl.program_id(0)

    x = x_ref[...]
    hb = _rms(x, g_ref[...]).astype(BF16)

    def proj(c0, n):
        return jnp.dot(hb, w_ref[0, :, c0:c0 + n], preferred_element_type=F32)

    def proj_t(wt_ref):
        return lax.dot_general(wt_ref[0], hb, (((1,), (1,)), ((), ())), preferred_element_type=F32)

    c0 = 2 * GM_WIDTH

    cv = proj(c0 + 3 * DA_WIDTH, 2 * CV_WIDTH)
    hg = cv[:, :CV_WIDTH] * jax.nn.sigmoid(cv[:, CV_WIDTH:])

    @pl.when(i == 0)
    def _():
        hbuf[0, 0:CV_HALO, :] = jnp.zeros((CV_HALO, CV_WIDTH), F32)

    hbuf[0, CV_HALO:CV_HALO + tm, :] = hg
    n_shift_rows = tm + CV_HALO - V7X_SUBLANES
    for b in range(1, V7X_SUBLANES):
        hbuf[b, 0:n_shift_rows, :] = hbuf[0, b:b + n_shift_rows, :]

    gm = _gelu(proj(0, 2 * GM_WIDTH))
    qvt = proj_t(wqvt_ref)
    qt = (qvt[0:DA_WIDTH, :] * (DA_QK_DIM ** -0.5 * LOG2E)).astype(BF16)
    qt_ref[:, 0, :, :] = qt.reshape(DA_HEADS, DA_V_DIM, tm)
    u = gm[:, :GM_WIDTH]
    v = gm[:, GM_WIDTH:]
    mu = jnp.mean(v, axis=-1, keepdims=True)
    d = v - mu
    var = jnp.mean(d * d, axis=-1, keepdims=True)
    vn = d * lax.rsqrt(var + EPS) * lng_ref[...] + lnb_ref[...]

    row = lax.broadcasted_iota(jnp.int32, (CHUNK, CHUNK), 0)
    col = lax.broadcasted_iota(jnp.int32, (CHUNK, CHUNK), 1)
    causal = row >= col
    wcat = jnp.concatenate(
        [jnp.where(causal, ws_ref[h], 0.0).astype(BF16) for h in range(GM_HEADS)], axis=1)
    head_of_lane = lax.broadcasted_iota(jnp.int32, (CHUNK, GM_WIDTH), 1) // GM_HEAD_DIM
    bs = bs_ref[...]
    for c in range(tm // CHUNK):
        r0 = c * CHUNK
        vc = vn[r0:r0 + CHUNK, :]
        vstack = jnp.concatenate(
            [jnp.where(head_of_lane == h, vc, 0.0).astype(BF16) for h in range(GM_HEADS)], axis=0)
        mixed = jnp.dot(wcat, vstack, preferred_element_type=F32) + bs
        a_ref[r0:r0 + CHUNK, :] = (u[r0:r0 + CHUNK, :] * mixed).astype(BF16)

    k_ref[...] = proj(c0 + DA_WIDTH, DA_WIDTH).astype(BF16)
    vt = qvt[DA_WIDTH:, :].astype(BF16).reshape(DA_HEADS, DA_V_DIM, tm)
    for kc in range(tm // V7X_LANES):
        vt_ref[:, 0, kc, :, :] = vt[:, :, kc * V7X_LANES:(kc + 1) * V7X_LANES]

    acc = jnp.broadcast_to(cb_ref[...], (tm, CV_WIDTH))
    for kk in range(CV_KERNEL):
        off = CV_HALO - (CV_KERNEL - 1) + kk
        b, a0 = off % V7X_SUBLANES, off - off % V7X_SUBLANES
        acc = acc + cw_ref[kk:kk + 1, :] * hbuf[b, a0:a0 + tm, :]
    hbuf[0, 0:CV_HALO, :] = hbuf[0, tm:tm + CV_HALO, :]

    gmu = _split_dot(acc, gmat_ref)
    dd = acc - gmu
    gvar = _split_dot(dd * dd, gmat_ref)
    y = dd * lax.rsqrt(gvar + EPS) * cg_ref[...] + cbeta_ref[...]
    c_ref[...] = (y * jax.nn.sigmoid(y)).astype(BF16)


def _mix_in(layer, x, g, w_bf, wqvt_bf, lng, lnb, ws, bs_full, cw, cb, cg, cbeta, gmat):
    seq = x.shape[0]
    tm = ROW_TILE
    full = lambda shape: pl.BlockSpec(shape, lambda i: (0,) * len(shape))
    of_layer = lambda shape: pl.BlockSpec((1,) + shape, lambda i: (layer,) + (0,) * len(shape))
    rows = lambda n: pl.BlockSpec((tm, n), lambda i: (i, 0))
    slabs = pl.BlockSpec((DA_HEADS, 1, DA_V_DIM, tm), lambda i: (0, i, 0, 0))
    slab_shape = jax.ShapeDtypeStruct((DA_HEADS, seq // tm, DA_V_DIM, tm), BF16)
    pieces = pl.BlockSpec((DA_HEADS, 1, tm // V7X_LANES, DA_V_DIM, V7X_LANES), lambda i: (0, i, 0, 0, 0))
    pieces_shape = jax.ShapeDtypeStruct((DA_HEADS, seq // tm, tm // V7X_LANES, DA_V_DIM, V7X_LANES), BF16)
    out_shapes = (
        jax.ShapeDtypeStruct((seq, GM_WIDTH), BF16),
        slab_shape,
        jax.ShapeDtypeStruct((seq, DA_WIDTH), BF16),
        pieces_shape,
        jax.ShapeDtypeStruct((seq, CV_WIDTH), BF16),
    )
    return pl.pallas_call(
        _mix_in_kernel,
        out_shape=out_shapes,
        grid=(seq // tm,),
        in_specs=[
            rows(D_MODEL),
            full((1, D_MODEL)),
            of_layer((D_MODEL, IN_WIDTH)),
            of_layer((2 * DA_WIDTH, D_MODEL)),
            full((1, GM_WIDTH)), full((1, GM_WIDTH)),
            full((GM_HEADS, CHUNK, CHUNK)),
            full((CHUNK, GM_WIDTH)),
            full((CV_HALO, CV_WIDTH)),
            full((1, CV_WIDTH)), full((1, CV_WIDTH)), full((1, CV_WIDTH)),
            full((CV_WIDTH, CV_WIDTH)),
        ],
        out_specs=(rows(GM_WIDTH), slabs, rows(DA_WIDTH), pieces, rows(CV_WIDTH)),
        scratch_shapes=[pltpu.VMEM((V7X_SUBLANES, CV_HALO + tm, CV_WIDTH), F32)],
        compiler_params=pltpu.CompilerParams(
            dimension_semantics=("arbitrary",),
            vmem_limit_bytes=V7X_SCOPED_VMEM_BYTES),
        name="mix_in",
    )(x, g, w_bf, wqvt_bf, lng, lnb, ws, bs_full, cw, cb, cg, cbeta, gmat)


def _fill_bias(rb_ref, bias_s, head, t):
    sub = V7X_LANES
    table = _bucket_table()
    first_rel = [int(np.argmax(table == b)) for b in range(N_BUCKETS)]
    last = rb_ref[N_BUCKETS - 1, head]
    vals = [(rb_ref[b, head] - last) * LOG2E for b in range(N_BUCKETS)]
    d0 = (lax.broadcasted_iota(jnp.int32, (sub, sub), 1)
          - lax.broadcasted_iota(jnp.int32, (sub, sub), 0))

    def band(base):
        rel = d0 + base
        v = jnp.full((sub, sub), vals[0], F32)
        for b in range(1, N_BUCKETS):
            v = jnp.where(rel >= first_rel[b], vals[b], v)
        return jnp.where(rel < 0, -jnp.inf, v)

    bands = {0: band(0), sub: band(sub)}
    for cb in range(bias_s.shape[0] // sub):
        for rb in range(bias_s.shape[1] // sub):
            base = sub * (rb - cb) + t
            if base in bands:
                tile = bands[base]
            elif base > sub:
                tile = jnp.zeros((sub, sub), F32)
            else:
                tile = jnp.full((sub, sub), -jnp.inf, F32)
            bias_s[cb * sub:(cb + 1) * sub, rb * sub:(rb + 1) * sub] = tile


def _diff_attn_kernel(rb_ref, qt_ref, k_ref, vt_ref, lq1_ref, lk1_ref, lq2_ref, lk2_ref, sg_ref,
                      o_ref, qst, acc, m_s, l_s, s_buf, mx_buf, bias_s, *, lambda_init):
    t = ATT_TILE
    tq = o_ref.shape[0]
    i = pl.program_id(1)

    @pl.when(i == 0)
    def _():
        _fill_bias(rb_ref, bias_s, pl.program_id(0), t)

    n_groups = 2 * tq // ATT_GROUP
    cols = [slice(g * ATT_GROUP, (g + 1) * ATT_GROUP) for g in range(n_groups)]
    groups_per_slab = t // ATT_GROUP
    for slab in range(ATT_Q_SLABS):
        qt = qt_ref[0, slab]
        chan = lax.broadcasted_iota(jnp.int32, qt.shape, 0)
        zero = jnp.zeros_like(qt)
        q1 = jnp.where(chan < DA_QK_DIM, qt, zero)
        q2 = jnp.where(chan >= DA_QK_DIM, qt, zero)
        for gg in range(groups_per_slab):
            g = slab * groups_per_slab + gg
            qst[g] = q1[:, gg * ATT_GROUP:(gg + 1) * ATT_GROUP]
            qst[n_groups // 2 + g] = q2[:, gg * ATT_GROUP:(gg + 1) * ATT_GROUP]
    m_s[...] = jnp.full(m_s.shape, -jnp.inf, F32)
    l_s[...] = jnp.zeros(l_s.shape, F32)
    acc[...] = jnp.zeros(acc.shape, F32)

    def group_kind(g, d):
        r0 = (g * ATT_GROUP) % tq
        rel_min = r0 + t * (1 - d) - (t - 1)
        rel_max = r0 + ATT_GROUP - 1 + t * (1 - d)
        if rel_max < 0:
            return "masked"
        return "plain" if rel_min >= MAX_DISTANCE else "biased"

    def visible_keys(g, d):
        if d is None:
            return t
        r0 = (g * ATT_GROUP) % tq
        n = min(t, r0 + ATT_GROUP + t * (1 - d))
        return -(-n // V7X_MXU_DIM) * V7X_MXU_DIM

    def scores(j, slot, d=None):
        kb = k_ref[pl.ds(pl.multiple_of(j * t, t), t), :]
        for g in range(n_groups):
            if d is not None and group_kind(g, d) == "masked":
                continue
            n = visible_keys(g, d)
            s = jnp.dot(kb[0:n, :], qst[g], preferred_element_type=F32)
            s_buf[slot, g, 0:n, :] = s
            mx_buf[slot, :, cols[g]] = jnp.max(s, axis=0, keepdims=True)

    def softmax_pv(j, slot, d=None):
        vt = jnp.concatenate([vt_ref[0, j, kc] for kc in range(t // V7X_LANES)], axis=1)
        for g in range(n_groups):
            kind = "plain" if d is None else group_kind(g, d)
            if kind == "masked":
                continue
            n = visible_keys(g, d)
            s = s_buf[slot, g, 0:n, :]
            if kind == "plain":
                s_max = mx_buf[slot, :, cols[g]]
            else:
                r0 = (g * ATT_GROUP) % tq
                s = s + bias_s[d * t:d * t + n, r0:r0 + ATT_GROUP]
                s_max = jnp.max(s, axis=0, keepdims=True)
            m_prev = m_s[:, cols[g]]
            m_new = jnp.maximum(m_prev, s_max)
            alpha = jnp.exp2(m_prev - m_new)
            p = jnp.exp2(s - m_new)
            l_s[:, cols[g]] = alpha * l_s[:, cols[g]] + jnp.sum(p, axis=0, keepdims=True)
            acc[g] = alpha * acc[g] + jnp.dot(vt[:, 0:n], p.astype(BF16), preferred_element_type=F32)
            m_s[:, cols[g]] = m_new

    first = ATT_Q_SLABS * i
    scores(0, 0)

    def pair(j, d_second, d_next):
        scores(j + 1, 1)
        softmax_pv(j, 0)
        scores(j + 2, 0, d_next)
        softmax_pv(j + 1, 1, d_second)

    def far_pair(pp, carry):
        pair(2 * pp, None, None)
        return carry

    lax.fori_loop(0, i - 1, far_pair, 0)

    @pl.when(i > 0)
    def _():
        pair(first - 2, 0, 1)

    scores(first + 1, 1, 2)
    softmax_pv(first, 0, 1)
    softmax_pv(first + 1, 1, 2)

    half = n_groups // 2
    inv_l = 1.0 / l_s[...]
    o1 = jnp.concatenate([acc[g] for g in range(half)], axis=1) * inv_l[:, 0:tq]
    o2 = jnp.concatenate([acc[half + g] for g in range(half)], axis=1) * inv_l[:, tq:2 * tq]
    lam = (jnp.exp(jnp.sum(lq1_ref[...] * lk1_ref[...], axis=-1, keepdims=True))
           - jnp.exp(jnp.sum(lq2_ref[...] * lk2_ref[...], axis=-1, keepdims=True))
           + lambda_init)
    ot = o1 - lam * o2
    ot = ot * lax.rsqrt(jnp.mean(ot * ot, axis=0, keepdims=True) + EPS)
    o_ref[...] = (jnp.transpose(ot) * sg_ref[...] * (1.0 - lambda_init)).astype(BF16)


def _diff_attn(rel_bias, qt, k, vt, lq1, lk1, lq2, lk2, sg, lambda_init):
    seq = k.shape[0]
    t = ATT_TILE
    tq = ATT_Q_SLABS * t
    vec = lambda n: pl.BlockSpec((1, n), lambda h, i: (0, 0))
    return pl.pallas_call(
        functools.partial(_diff_attn_kernel, lambda_init=lambda_init),
        out_shape=jax.ShapeDtypeStruct((seq, DA_WIDTH), BF16),
        grid=(DA_HEADS, seq // tq),
        in_specs=[
            pl.BlockSpec(memory_space=pltpu.SMEM),
            pl.BlockSpec((1, ATT_Q_SLABS, DA_V_DIM, t), lambda h, i: (h, i, 0, 0)),
            pl.BlockSpec((seq, DA_V_DIM), lambda h, i: (0, h)),
            pl.BlockSpec((1, seq // t, t // V7X_LANES, DA_V_DIM, V7X_LANES), lambda h, i: (h, 0, 0, 0, 0)),
            vec(DA_QK_DIM), vec(DA_QK_DIM), vec(DA_QK_DIM), vec(DA_QK_DIM),
            vec(DA_V_DIM),
        ],
        out_specs=pl.BlockSpec((tq, DA_V_DIM), lambda h, i: (i, h)),
        scratch_shapes=[
            pltpu.VMEM((2 * tq // ATT_GROUP, 2 * DA_QK_DIM, ATT_GROUP), BF16),
            pltpu.VMEM((2 * tq // ATT_GROUP, DA_V_DIM, ATT_GROUP), F32),
            pltpu.VMEM((1, 2 * tq), F32),
            pltpu.VMEM((1, 2 * tq), F32),
            pltpu.VMEM((2, 2 * tq // ATT_GROUP, t, ATT_GROUP), F32),
            pltpu.VMEM((2, 1, 2 * tq), F32),
            pltpu.VMEM(((ATT_Q_SLABS + 1) * t, tq), F32),
        ],
        compiler_params=pltpu.CompilerParams(
            dimension_semantics=("arbitrary", "arbitrary"),
            vmem_limit_bytes=V7X_SCOPED_VMEM_BYTES),
        name="diff_attn",
    )(rel_bias, qt, k, vt, lq1, lk1, lq2, lk2, sg)


def _out_ffn_kernel(x_ref, a_ref, b_ref, c_ref, wo_ref, gpm_ref, gpf_ref, gqf_ref,
                    wu_ref, cw_ref, cb_ref, wd_ref, xo_ref, carry, ubuf0, ubuf1, yacc):
    tm = x_ref.shape[0]
    i = pl.program_id(0)
    n_chunks = D_FF // FF_CHUNK

    @pl.when(i == 0)
    def _():
        carry[...] = jnp.zeros(carry.shape, F32)

    o_a, o_b, o_c = 0, GM_WIDTH, GM_WIDTH + DA_WIDTH
    mix = (jnp.dot(a_ref[...], wo_ref[0, o_a:o_b, :], preferred_element_type=F32)
           + jnp.dot(b_ref[...], wo_ref[0, o_b:o_c, :], preferred_element_type=F32)
           + jnp.dot(c_ref[...], wo_ref[0, o_c:, :], preferred_element_type=F32))
    x1 = x_ref[...] + _rms(mix, gpm_ref[...])
    hb = _rms(x1, gpf_ref[...]).astype(BF16)

    ub = tm // FFN_UP_SPLIT
    rb = tm // FFN_ROW_SPLIT

    def up_proj(ch, r):
        bufs = (ubuf0, ubuf1)[ch % 2]
        r0 = r * ub
        for half in range(2):
            col = half * D_FF + ch * FF_CHUNK
            up = jnp.dot(hb[r0:r0 + ub, :], wu_ref[0, :, col:col + FF_CHUNK], preferred_element_type=F32)
            if r == 0:
                bufs[half, 0:FFN_HALO, :] = carry[:, col:col + FF_CHUNK]
            bufs[half, FFN_HALO + r0:FFN_HALO + r0 + ub, :] = up
            if r == FFN_UP_SPLIT - 1:
                carry[:, col:col + FF_CHUNK] = up[ub - FFN_HALO:ub, :]

    def conv(ch, half, r):
        bufs = (ubuf0, ubuf1)[ch % 2]
        col = half * D_FF + ch * FF_CHUNK
        out = cb_ref[:, col:col + FF_CHUNK]
        for kk in range(FFN_KERNEL):
            off = FFN_HALO - (FFN_KERNEL - 1) + kk + r * rb
            out = out + cw_ref[kk:kk + 1, col:col + FF_CHUNK] * bufs[half, off:off + rb, :]
        return out

    for r in range(FFN_UP_SPLIT):
        up_proj(0, r)
    for ch in range(n_chunks):
        col = ch * FF_CHUNK
        for r in range(FFN_ROW_SPLIT):
            if ch + 1 < n_chunks and r % (FFN_ROW_SPLIT // FFN_UP_SPLIT) == 0:
                up_proj(ch + 1, r // (FFN_ROW_SPLIT // FFN_UP_SPLIT))
            act = (_gelu(conv(ch, 0, r)) * conv(ch, 1, r)).astype(BF16)
            contrib = jnp.dot(act, wd_ref[0, col:col + FF_CHUNK, :], preferred_element_type=F32)
            rows = slice(r * rb, (r + 1) * rb)
            if ch == 0:
                yacc[rows, :] = contrib
            else:
                yacc[rows, :] += contrib

    xo_ref[...] = x1 + _rms(yacc[...], gqf_ref[...])


def _out_ffn(layer, x, a, b, c, wo_bf, gpm, gpf, gqf, wu_bf, cw, cb, wd_bf):
    seq = x.shape[0]
    tm = ROW_TILE
    rows = lambda n: pl.BlockSpec((tm, n), lambda i: (i, 0))

    def resident(shape):
        return pl.BlockSpec(shape, lambda i: (0,) * len(shape), pipeline_mode=pl.Buffered(1))

    def resident_of_layer(shape):
        return pl.BlockSpec((1,) + shape, lambda i: (layer,) + (0,) * len(shape),
                            pipeline_mode=pl.Buffered(1))

    return pl.pallas_call(
        _out_ffn_kernel,
        out_shape=jax.ShapeDtypeStruct((seq, D_MODEL), F32),
        grid=(seq // tm,),
        in_specs=[
            rows(D_MODEL), rows(GM_WIDTH), rows(DA_WIDTH), rows(CV_WIDTH),
            resident_of_layer((D_MODEL, D_MODEL)),
            resident((1, D_MODEL)), resident((1, D_MODEL)), resident((1, D_MODEL)),
            resident_of_layer((D_MODEL, 2 * D_FF)),
            resident((V7X_SUBLANES, 2 * D_FF)),
            resident((1, 2 * D_FF)),
            resident_of_layer((D_FF, D_MODEL)),
        ],
        out_specs=rows(D_MODEL),
        scratch_shapes=[
            pltpu.VMEM((FFN_HALO, 2 * D_FF), F32),
            pltpu.VMEM((2, FFN_HALO + tm, FF_CHUNK), F32),
            pltpu.VMEM((2, FFN_HALO + tm, FF_CHUNK), F32),
            pltpu.VMEM((tm, D_MODEL), F32),
        ],
        compiler_params=pltpu.CompilerParams(
            dimension_semantics=("arbitrary",),
            vmem_limit_bytes=V7X_SCOPED_VMEM_BYTES),
        name="out_ffn",
    )(x, a, b, c, wo_bf, gpm, gpf, gqf, wu_bf, cw, cb, wd_bf)


def _bucket_table():
    n = np.arange(MAX_DISTANCE + 1)
    max_exact = N_BUCKETS // 2
    nf = np.maximum(n, 1).astype(np.float32)
    large = max_exact + (np.log(nf / max_exact) / math.log(MAX_DISTANCE / max_exact)
                         * (N_BUCKETS - max_exact)).astype(np.int32)
    large = np.minimum(large, N_BUCKETS - 1)
    return np.where(n < max_exact, n, large).astype(np.int32)


def _group_mean_matrix():
    g = np.arange(CV_WIDTH) // (CV_WIDTH // CV_GROUPS)
    return jnp.asarray((g[:, None] == g[None, :]) / (CV_WIDTH // CV_GROUPS), dtype=BF16)


def _pad_rows(w, rows):
    return jnp.concatenate([w, jnp.zeros((rows - w.shape[0], w.shape[1]), w.dtype)], axis=0)


def kernel(x, w_in, w_out, gm_ln_g, gm_ln_b, gm_w_s, gm_b_s, da_lq1, da_lk1, da_lq2, da_lk2, da_subln_g, rel_bias, cv_dw_w, cv_dw_b, cv_ln_g, cv_ln_b, ffn_w_up, ffn_conv_w, ffn_conv_b, ffn_w_down, pre_mix_g, post_mix_g, pre_ffn_g, post_ffn_g):
    batch, seq, d_model = x.shape
    depth = w_in.shape[0]
    assert batch == 1 and d_model == D_MODEL and w_in.shape[2] == IN_WIDTH
    assert seq % ROW_TILE == 0 and ROW_TILE == ATT_TILE and ATT_TILE >= MAX_DISTANCE
    assert ATT_Q_SLABS == 2 and seq % (ATT_Q_SLABS * ATT_TILE) == 0
    assert ROW_TILE % CHUNK == 0 and D_FF % FF_CHUNK == 0 and CV_HALO >= CV_KERNEL - 1

    gmat = _group_mean_matrix()
    row = lambda p: p.reshape(1, -1)

    w_in_bf = w_in.astype(BF16)
    q0, v0 = 2 * GM_WIDTH, 2 * GM_WIDTH + 2 * DA_WIDTH
    wqvt_bf = jnp.transpose(
        jnp.concatenate([w_in[:, :, q0:q0 + DA_WIDTH], w_in[:, :, v0:v0 + DA_WIDTH]], axis=2),
        (0, 2, 1)).astype(BF16)
    w_out_bf = w_out.astype(BF16)
    w_up_bf = ffn_w_up.astype(BF16)
    w_down_bf = ffn_w_down.astype(BF16)

    xs = x[0]
    for l in range(depth):
        lambda_init = 0.8 - 0.6 * math.exp(-0.3 * l)
        bs_full = jnp.repeat(jnp.transpose(gm_b_s[l]), GM_HEAD_DIM, axis=1)
        a, q, k, v, c = _mix_in(
            l, xs, row(pre_mix_g[l]), w_in_bf, wqvt_bf,
            row(gm_ln_g[l]), row(gm_ln_b[l]), gm_w_s[l], bs_full,
            _pad_rows(cv_dw_w[l], CV_HALO), row(cv_dw_b[l]), row(cv_ln_g[l]), row(cv_ln_b[l]), gmat)
        b = _diff_attn(rel_bias, q, k, v, row(da_lq1[l]), row(da_lk1[l]), row(da_lq2[l]), row(da_lk2[l]),
                       row(da_subln_g[l]), lambda_init)
        xs = _out_ffn(
            l, xs, a, b, c, w_out_bf,
            row(post_mix_g[l]), row(pre_ffn_g[l]), row(post_ffn_g[l]),
            w_up_bf, _pad_rows(ffn_conv_w[l], V7X_SUBLANES), row(ffn_conv_b[l]),
            w_down_bf)
    return xs[None]
```

```python
import functools
import math

import numpy as np
import jax
import jax.numpy as jnp
from jax import lax
from jax.experimental import pallas as pl
from jax.experimental.pallas import tpu as pltpu

F32 = jnp.float32
BF16 = jnp.bfloat16

D_MODEL = 1024
GM_HEADS = 4
GM_WIDTH = 256
GM_HEAD_DIM = GM_WIDTH // GM_HEADS
CHUNK = 128
DA_HEADS = 4
DA_WIDTH = 512
DA_V_DIM = 128
DA_QK_DIM = 64
CV_GROUPS = 4
CV_WIDTH = 256
CV_KERNEL = 31
D_FF = 2816
FFN_KERNEL = 3
N_BUCKETS = 32
MAX_DISTANCE = 128
EPS = 1e-6
IN_WIDTH = 2 * GM_WIDTH + 3 * DA_WIDTH + 2 * CV_WIDTH

V7X_SUBLANES = 8
V7X_LANES = 128
V7X_MXU_DIM = 256
V7X_SCOPED_VMEM_BYTES = 60000 * 1024

ROW_TILE = 512
ATT_TILE = 512
ATT_Q_SLABS = 2
ATT_GROUP = V7X_MXU_DIM
CV_HALO = 32
FFN_HALO = V7X_SUBLANES
FF_CHUNK = V7X_MXU_DIM
FFN_UP_SPLIT = 1
FFN_ROW_SPLIT = 1

LOG2E = math.log2(math.e)


_GELU_K1 = -2.0 * math.sqrt(2.0 / math.pi) * LOG2E
_GELU_K3 = _GELU_K1 * 0.044715


def _gelu(x):
    return x / (1.0 + jnp.exp2(x * (_GELU_K1 + _GELU_K3 * (x * x))))


def _rms(x, g):
    return x * lax.rsqrt(jnp.mean(x * x, axis=-1, keepdims=True) + EPS) * g


def _split_dot(x, g_ref):
    hi = x.astype(BF16)
    lo = (x - hi.astype(F32)).astype(BF16)
    g = g_ref[...]
    return (jnp.dot(hi, g, preferred_element_type=F32)
            + jnp.dot(lo, g, preferred_element_type=F32))


def _mix_in_kernel(x_ref, g_ref, w_ref, wqvt_ref, lng_ref, lnb_ref, ws_ref, bs_ref,
                   cw_ref, cb_ref, cg_ref, cbeta_ref, gmat_ref,
                   a_ref, qt_ref, k_ref, vt_ref, c_ref, hbuf):
    tm = x_ref.shape[0]
    i = pl.program_id(0)

    x = x_ref[...]
    hb = _rms(x, g_ref[...]).astype(BF16)

    def proj(c0, n):
        return jnp.dot(hb, w_ref[0, :, c0:c0 + n], preferred_element_type=F32)

    def proj_t(wt_ref):
        return lax.dot_general(wt_ref[0], hb, (((1,), (1,)), ((), ())), preferred_element_type=F32)

    c0 = 2 * GM_WIDTH

    cv = proj(c0 + 3 * DA_WIDTH, 2 * CV_WIDTH)
    hg = cv[:, :CV_WIDTH] * jax.nn.sigmoid(cv[:, CV_WIDTH:])

    @pl.when(i == 0)
    def _():
        hbuf[0, 0:CV_HALO, :] = jnp.zeros((CV_HALO, CV_WIDTH), F32)

    hbuf[0, CV_HALO:CV_HALO + tm, :] = hg
    n_shift_rows = tm + CV_HALO - V7X_SUBLANES
    for b in range(1, V7X_SUBLANES):
        hbuf[b, 0:n_shift_rows, :] = hbuf[0, b:b + n_shift_rows, :]

    gm = _gelu(proj(0, 2 * GM_WIDTH))
    qvt = proj_t(wqvt_ref)
    qt = (qvt[0:DA_WIDTH, :] * (DA_QK_DIM ** -0.5 * LOG2E)).astype(BF16)
    qt_ref[:, 0, :, :] = qt.reshape(DA_HEADS, DA_V_DIM, tm)
    u = gm[:, :GM_WIDTH]
    v = gm[:, GM_WIDTH:]
    mu = jnp.mean(v, axis=-1, keepdims=True)
    d = v - mu
    var = jnp.mean(d * d, axis=-1, keepdims=True)
    vn = d * lax.rsqrt(var + EPS) * lng_ref[...] + lnb_ref[...]

    row = lax.broadcasted_iota(jnp.int32, (CHUNK, CHUNK), 0)
    col = lax.broadcasted_iota(jnp.int32, (CHUNK, CHUNK), 1)
    causal = row >= col
    wcat = jnp.concatenate(
        [jnp.where(causal, ws_ref[h], 0.0).astype(BF16) for h in range(GM_HEADS)], axis=1)
    head_of_lane = lax.broadcasted_iota(jnp.int32, (CHUNK, GM_WIDTH), 1) // GM_HEAD_DIM
    bs = bs_ref[...]
    for c in range(tm // CHUNK):
        r0 = c * CHUNK
        vc = vn[r0:r0 + CHUNK, :]
        vstack = jnp.concatenate(
            [jnp.where(head_of_lane == h, vc, 0.0).astype(BF16) for h in range(GM_HEADS)], axis=0)
        mixed = jnp.dot(wcat, vstack, preferred_element_type=F32) + bs
        a_ref[r0:r0 + CHUNK, :] = (u[r0:r0 + CHUNK, :] * mixed).astype(BF16)

    k_ref[...] = proj(c0 + DA_WIDTH, DA_WIDTH).astype(BF16)
    vt = qvt[DA_WIDTH:, :].astype(BF16).reshape(DA_HEADS, DA_V_DIM, tm)
    for kc in range(tm // V7X_LANES):
        vt_ref[:, 0, kc, :, :] = vt[:, :, kc * V7X_LANES:(kc + 1) * V7X_LANES]

    acc = jnp.broadcast_to(cb_ref[...], (tm, CV_WIDTH))
    for kk in range(CV_KERNEL):
        off = CV_HALO - (CV_KERNEL - 1) + kk
        b, a0 = off % V7X_SUBLANES, off - off % V7X_SUBLANES
        acc = acc + cw_ref[kk:kk + 1, :] * hbuf[b, a0:a0 + tm, :]
    hbuf[0, 0:CV_HALO, :] = hbuf[0, tm:tm + CV_HALO, :]

    gmu = _split_dot(acc, gmat_ref)
    dd = acc - gmu
    gvar = _split_dot(dd * dd, gmat_ref)
    y = dd * lax.rsqrt(gvar + EPS) * cg_ref[...] + cbeta_ref[...]
    c_ref[...] = (y * jax.nn.sigmoid(y)).astype(BF16)


def _mix_in(layer, x, g, w_bf, wqvt_bf, lng, lnb, ws, bs_full, cw, cb, cg, cbeta, gmat):
    seq = x.shape[0]
    tm = ROW_TILE
    full = lambda shape: pl.BlockSpec(shape, lambda i: (0,) * len(shape))
    of_layer = lambda shape: pl.BlockSpec((1,) + shape, lambda i: (layer,) + (0,) * len(shape))
    rows = lambda n: pl.BlockSpec((tm, n), lambda i: (i, 0))
    slabs = pl.BlockSpec((DA_HEADS, 1, DA_V_DIM, tm), lambda i: (0, i, 0, 0))
    slab_shape = jax.ShapeDtypeStruct((DA_HEADS, seq // tm, DA_V_DIM, tm), BF16)
    pieces = pl.BlockSpec((DA_HEADS, 1, tm // V7X_LANES, DA_V_DIM, V7X_LANES), lambda i: (0, i, 0, 0, 0))
    pieces_shape = jax.ShapeDtypeStruct((DA_HEADS, seq // tm, tm // V7X_LANES, DA_V_DIM, V7X_LANES), BF16)
    out_shapes = (
        jax.ShapeDtypeStruct((seq, GM_WIDTH), BF16),
        slab_shape,
        jax.ShapeDtypeStruct((seq, DA_WIDTH), BF16),
        pieces_shape,
        jax.ShapeDtypeStruct((seq, CV_WIDTH), BF16),
    )
    return pl.pallas_call(
        _mix_in_kernel,
        out_shape=out_shapes,
        grid=(seq // tm,),
        in_specs=[
            rows(D_MODEL),
            full((1, D_MODEL)),
            of_layer((D_MODEL, IN_WIDTH)),
            of_layer((2 * DA_WIDTH, D_MODEL)),
            full((1, GM_WIDTH)), full((1, GM_WIDTH)),
            full((GM_HEADS, CHUNK, CHUNK)),
            full((CHUNK, GM_WIDTH)),
            full((CV_HALO, CV_WIDTH)),
            full((1, CV_WIDTH)), full((1, CV_WIDTH)), full((1, CV_WIDTH)),
            full((CV_WIDTH, CV_WIDTH)),
        ],
        out_specs=(rows(GM_WIDTH), slabs, rows(DA_WIDTH), pieces, rows(CV_WIDTH)),
        scratch_shapes=[pltpu.VMEM((V7X_SUBLANES, CV_HALO + tm, CV_WIDTH), F32)],
        compiler_params=pltpu.CompilerParams(
            dimension_semantics=("arbitrary",),
            vmem_limit_bytes=V7X_SCOPED_VMEM_BYTES),
        name="mix_in",
    )(x, g, w_bf, wqvt_bf, lng, lnb, ws, bs_full, cw, cb, cg, cbeta, gmat)


def _fill_bias(rb_ref, bias_s, head, t):
    sub = V7X_LANES
    table = _bucket_table()
    first_rel = [int(np.argmax(table == b)) for b in range(N_BUCKETS)]
    last = rb_ref[N_BUCKETS - 1, head]
    vals = [(rb_ref[b, head] - last) * LOG2E for b in range(N_BUCKETS)]
    d0 = (lax.broadcasted_iota(jnp.int32, (sub, sub), 1)
          - lax.broadcasted_iota(jnp.int32, (sub, sub), 0))

    def band(base):
        rel = d0 + base
        v = jnp.full((sub, sub), vals[0], F32)
        for b in range(1, N_BUCKETS):
            v = jnp.where(rel >= first_rel[b], vals[b], v)
        return jnp.where(rel < 0, -jnp.inf, v)

    bands = {0: band(0), sub: band(sub)}
    for cb in range(bias_s.shape[0] // sub):
        for rb in range(bias_s.shape[1] // sub):
            base = sub * (rb - cb) + t
            if base in bands:
                tile = bands[base]
            elif base > sub:
                tile = jnp.zeros((sub, sub), F32)
            else:
                tile = jnp.full((sub, sub), -jnp.inf, F32)
            bias_s[cb * sub:(cb + 1) * sub, rb * sub:(rb + 1) * sub] = tile


def _diff_attn_kernel(rb_ref, qt_ref, k_ref, vt_ref, lq1_ref, lk1_ref, lq2_ref, lk2_ref, sg_ref,
                      o_ref, qst, acc, m_s, l_s, s_buf, mx_buf, bias_s, *, lambda_init):
    t = ATT_TILE
    tq = o_ref.shape[0]
    i = pl.program_id(1)

    @pl.when(i == 0)
    def _():
        _fill_bias(rb_ref, bias_s, pl.program_id(0), t)

    n_groups = 2 * tq // ATT_GROUP
    cols = [slice(g * ATT_GROUP, (g + 1) * ATT_GROUP) for g in range(n_groups)]
    groups_per_slab = t // ATT_GROUP
    for slab in range(ATT_Q_SLABS):
        qt = qt_ref[0, slab]
        chan = lax.broadcasted_iota(jnp.int32, qt.shape, 0)
        zero = jnp.zeros_like(qt)
        q1 = jnp.where(chan < DA_QK_DIM, qt, zero)
        q2 = jnp.where(chan >= DA_QK_DIM, qt, zero)
        for gg in range(groups_per_slab):
            g = slab * groups_per_slab + gg
            qst[g] = q1[:, gg * ATT_GROUP:(gg + 1) * ATT_GROUP]
            qst[n_groups // 2 + g] = q2[:, gg * ATT_GROUP:(gg + 1) * ATT_GROUP]
    m_s[...] = jnp.full(m_s.shape, -jnp.inf, F32)
    l_s[...] = jnp.zeros(l_s.shape, F32)
    acc[...] = jnp.zeros(acc.shape, F32)

    def group_kind(g, d):
        r0 = (g * ATT_GROUP) % tq
        rel_min = r0 + t * (1 - d) - (t - 1)
        rel_max = r0 + ATT_GROUP - 1 + t * (1 - d)
        if rel_max < 0:
            return "masked"
        return "plain" if rel_min >= MAX_DISTANCE else "biased"

    def visible_keys(g, d):
        if d is None:
            return t
        r0 = (g * ATT_GROUP) % tq
        n = min(t, r0 + ATT_GROUP + t * (1 - d))
        return -(-n // V7X_MXU_DIM) * V7X_MXU_DIM

    def scores(j, slot, d=None):
        kb = k_ref[pl.ds(pl.multiple_of(j * t, t), t), :]
        for g in range(n_groups):
            if d is not None and group_kind(g, d) == "masked":
                continue
            n = visible_keys(g, d)
            s = jnp.dot(kb[0:n, :], qst[g], preferred_element_type=F32)
            s_buf[slot, g, 0:n, :] = s
            mx_buf[slot, :, cols[g]] = jnp.max(s, axis=0, keepdims=True)

    def softmax_pv(j, slot, d=None):
        vt = jnp.concatenate([vt_ref[0, j, kc] for kc in range(t // V7X_LANES)], axis=1)
        for g in range(n_groups):
            kind = "plain" if d is None else group_kind(g, d)
            if kind == "masked":
                continue
            n = visible_keys(g, d)
            s = s_buf[slot, g, 0:n, :]
            if kind == "plain":
                s_max = mx_buf[slot, :, cols[g]]
            else:
                r0 = (g * ATT_GROUP) % tq
                s = s + bias_s[d * t:d * t + n, r0:r0 + ATT_GROUP]
                s_max = jnp.max(s, axis=0, keepdims=True)
            m_prev = m_s[:, cols[g]]
            m_new = jnp.maximum(m_prev, s_max)
            alpha = jnp.exp2(m_prev - m_new)
            p = jnp.exp2(s - m_new)
            l_s[:, cols[g]] = alpha * l_s[:, cols[g]] + jnp.sum(p, axis=0, keepdims=True)
            acc[g] = alpha * acc[g] + jnp.dot(vt[:, 0:n], p.astype(BF16), preferred_element_type=F32)
            m_s[:, cols[g]] = m_new

    first = ATT_Q_SLABS * i
    scores(0, 0)

    def pair(j, d_second, d_next):
        scores(j + 1, 1)
        softmax_pv(j, 0)
        scores(j + 2, 0, d_next)
        softmax_pv(j + 1, 1, d_second)

    def far_pair(pp, carry):
        pair(2 * pp, None, None)
        return carry

    lax.fori_loop(0, i - 1, far_pair, 0)

    @pl.when(i > 0)
    def _():
        pair(first - 2, 0, 1)

    scores(first + 1, 1, 2)
    softmax_pv(first, 0, 1)
    softmax_pv(first + 1, 1, 2)

    half = n_groups // 2
    inv_l = 1.0 / l_s[...]
    o1 = jnp.concatenate([acc[g] for g in range(half)], axis=1) * inv_l[:, 0:tq]
    o2 = jnp.concatenate([acc[half + g] for g in range(half)], axis=1) * inv_l[:, tq:2 * tq]
    lam = (jnp.exp(jnp.sum(lq1_ref[...] * lk1_ref[...], axis=-1, keepdims=True))
           - jnp.exp(jnp.sum(lq2_ref[...] * lk2_ref[...], axis=-1, keepdims=True))
           + lambda_init)
    ot = o1 - lam * o2
    ot = ot * lax.rsqrt(jnp.mean(ot * ot, axis=0, keepdims=True) + EPS)
    o_ref[...] = (jnp.transpose(ot) * sg_ref[...] * (1.0 - lambda_init)).astype(BF16)


def _diff_attn(rel_bias, qt, k, vt, lq1, lk1, lq2, lk2, sg, lambda_init):
    seq = k.shape[0]
    t = ATT_TILE
    tq = ATT_Q_SLABS * t
    vec = lambda n: pl.BlockSpec((1, n), lambda h, i: (0, 0))
    return pl.pallas_call(
        functools.partial(_diff_attn_kernel, lambda_init=lambda_init),
        out_shape=jax.ShapeDtypeStruct((seq, DA_WIDTH), BF16),
        grid=(DA_HEADS, seq // tq),
        in_specs=[
            pl.BlockSpec(memory_space=pltpu.SMEM),
            pl.BlockSpec((1, ATT_Q_SLABS, DA_V_DIM, t), lambda h, i: (h, i, 0, 0)),
            pl.BlockSpec((seq, DA_V_DIM), lambda h, i: (0, h)),
            pl.BlockSpec((1, seq // t, t // V7X_LANES, DA_V_DIM, V7X_LANES), lambda h, i: (h, 0, 0, 0, 0)),
            vec(DA_QK_DIM), vec(DA_QK_DIM), vec(DA_QK_DIM), vec(DA_QK_DIM),
            vec(DA_V_DIM),
        ],
        out_specs=pl.BlockSpec((tq, DA_V_DIM), lambda h, i: (i, h)),
        scratch_shapes=[
            pltpu.VMEM((2 * tq // ATT_GROUP, 2 * DA_QK_DIM, ATT_GROUP), BF16),
            pltpu.VMEM((2 * tq // ATT_GROUP, DA_V_DIM, ATT_GROUP), F32),
            pltpu.VMEM((1, 2 * tq), F32),
            pltpu.VMEM((1, 2 * tq), F32),
            pltpu.VMEM((2, 2 * tq // ATT_GROUP, t, ATT_GROUP), F32),
            pltpu.VMEM((2, 1, 2 * tq), F32),
            pltpu.VMEM(((ATT_Q_SLABS + 1) * t, tq), F32),
        ],
        compiler_params=pltpu.CompilerParams(
            dimension_semantics=("arbitrary", "arbitrary"),
            vmem_limit_bytes=V7X_SCOPED_VMEM_BYTES),
        name="diff_attn",
    )(rel_bias, qt, k, vt, lq1, lk1, lq2, lk2, sg)


def _out_ffn_kernel(x_ref, a_ref, b_ref, c_ref, wo_ref, gpm_ref, gpf_ref, gqf_ref,
                    wu_ref, cw_ref, cb_ref, wd_ref, xo_ref, carry, ubuf, yacc):
    tm = x_ref.shape[0]
    i = pl.program_id(0)
    n_chunks = D_FF // FF_CHUNK

    @pl.when(i == 0)
    def _():
        carry[...] = jnp.zeros(carry.shape, F32)

    o_a, o_b, o_c = 0, GM_WIDTH, GM_WIDTH + DA_WIDTH
    mix = (jnp.dot(a_ref[...], wo_ref[0, o_a:o_b, :], preferred_element_type=F32)
           + jnp.dot(b_ref[...], wo_ref[0, o_b:o_c, :], preferred_element_type=F32)
           + jnp.dot(c_ref[...], wo_ref[0, o_c:, :], preferred_element_type=F32))
    x1 = x_ref[...] + _rms(mix, gpm_ref[...])
    hb = _rms(x1, gpf_ref[...]).astype(BF16)

    def up_proj(ch, parity):
        for half in range(2):
            idx = half * n_chunks + ch
            up = jnp.dot(hb, wu_ref[0, idx], preferred_element_type=F32)
            ubuf[parity, half, 0:FFN_HALO, :] = carry[idx]
            ubuf[parity, half, FFN_HALO:FFN_HALO + tm, :] = up
            carry[idx] = up[tm - FFN_HALO:tm, :]

    def conv(ch, parity, half):
        idx = half * n_chunks + ch
        out = cb_ref[idx]
        for kk in range(FFN_KERNEL):
            off = FFN_HALO - (FFN_KERNEL - 1) + kk
            out = out + cw_ref[idx, kk:kk + 1, :] * ubuf[parity, half, off:off + tm, :]
        return out

    def down_proj(ch, parity):
        act = (_gelu(conv(ch, parity, 0)) * conv(ch, parity, 1)).astype(BF16)
        yacc[...] += jnp.dot(act, wd_ref[0, ch], preferred_element_type=F32)

    yacc[...] = jnp.zeros(yacc.shape, F32)
    up_proj(0, 0)

    def chunk_step(ch, c):
        parity = ch % 2
        up_proj(ch + 1, 1 - parity)
        down_proj(ch, parity)
        return c

    lax.fori_loop(0, n_chunks - 1, chunk_step, 0, unroll=2)
    down_proj(n_chunks - 1, (n_chunks - 1) % 2)

    xo_ref[...] = x1 + _rms(yacc[...], gqf_ref[...])


def _out_ffn(layer, x, a, b, c, wo_bf, gpm, gpf, gqf, wu_bf, cw, cb, wd_bf):
    seq = x.shape[0]
    tm = ROW_TILE
    n_chunks = D_FF // FF_CHUNK
    rows = lambda n: pl.BlockSpec((tm, n), lambda i: (i, 0))

    def resident(shape):
        return pl.BlockSpec(shape, lambda i: (0,) * len(shape), pipeline_mode=pl.Buffered(1))

    def resident_of_layer(shape):
        return pl.BlockSpec((1,) + shape, lambda i: (layer,) + (0,) * len(shape),
                            pipeline_mode=pl.Buffered(1))

    return pl.pallas_call(
        _out_ffn_kernel,
        out_shape=jax.ShapeDtypeStruct((seq, D_MODEL), F32),
        grid=(seq // tm,),
        in_specs=[
            rows(D_MODEL), rows(GM_WIDTH), rows(DA_WIDTH), rows(CV_WIDTH),
            resident_of_layer((D_MODEL, D_MODEL)),
            resident((1, D_MODEL)), resident((1, D_MODEL)), resident((1, D_MODEL)),
            resident_of_layer((2 * n_chunks, D_MODEL, FF_CHUNK)),
            resident((2 * n_chunks, V7X_SUBLANES, FF_CHUNK)),
            resident((2 * n_chunks, 1, FF_CHUNK)),
            resident_of_layer((n_chunks, FF_CHUNK, D_MODEL)),
        ],
        out_specs=rows(D_MODEL),
        scratch_shapes=[
            pltpu.VMEM((2 * n_chunks, FFN_HALO, FF_CHUNK), F32),
            pltpu.VMEM((2, 2, FFN_HALO + tm, FF_CHUNK), F32),
            pltpu.VMEM((tm, D_MODEL), F32),
        ],
        compiler_params=pltpu.CompilerParams(
            dimension_semantics=("arbitrary",),
            vmem_limit_bytes=V7X_SCOPED_VMEM_BYTES),
        name="out_ffn",
    )(x, a, b, c, wo_bf, gpm, gpf, gqf, wu_bf, cw, cb, wd_bf)


def _bucket_table():
    n = np.arange(MAX_DISTANCE + 1)
    max_exact = N_BUCKETS // 2
    nf = np.maximum(n, 1).astype(np.float32)
    large = max_exact + (np.log(nf / max_exact) / math.log(MAX_DISTANCE / max_exact)
                         * (N_BUCKETS - max_exact)).astype(np.int32)
    large = np.minimum(large, N_BUCKETS - 1)
    return np.where(n < max_exact, n, large).astype(np.int32)


def _group_mean_matrix():
    g = np.arange(CV_WIDTH) // (CV_WIDTH // CV_GROUPS)
    return jnp.asarray((g[:, None] == g[None, :]) / (CV_WIDTH // CV_GROUPS), dtype=BF16)


def _pad_rows(w, rows):
    return jnp.concatenate([w, jnp.zeros((rows - w.shape[0], w.shape[1]), w.dtype)], axis=0)


def kernel(x, w_in, w_out, gm_ln_g, gm_ln_b, gm_w_s, gm_b_s, da_lq1, da_lk1, da_lq2, da_lk2, da_subln_g, rel_bias, cv_dw_w, cv_dw_b, cv_ln_g, cv_ln_b, ffn_w_up, ffn_conv_w, ffn_conv_b, ffn_w_down, pre_mix_g, post_mix_g, pre_ffn_g, post_ffn_g):
    batch, seq, d_model = x.shape
    depth = w_in.shape[0]
    assert batch == 1 and d_model == D_MODEL and w_in.shape[2] == IN_WIDTH
    assert seq % ROW_TILE == 0 and ROW_TILE == ATT_TILE and ATT_TILE >= MAX_DISTANCE
    assert ATT_Q_SLABS == 2 and seq % (ATT_Q_SLABS * ATT_TILE) == 0
    assert ROW_TILE % CHUNK == 0 and D_FF % FF_CHUNK == 0 and CV_HALO >= CV_KERNEL - 1

    gmat = _group_mean_matrix()
    row = lambda p: p.reshape(1, -1)

    w_in_bf = w_in.astype(BF16)
    q0, v0 = 2 * GM_WIDTH, 2 * GM_WIDTH + 2 * DA_WIDTH
    wqvt_bf = jnp.transpose(
        jnp.concatenate([w_in_bf[:, :, q0:q0 + DA_WIDTH], w_in_bf[:, :, v0:v0 + DA_WIDTH]], axis=2), (0, 2, 1))
    w_out_bf = w_out.astype(BF16)
    n_ff = 2 * D_FF // FF_CHUNK
    w_up_bf = jnp.transpose(ffn_w_up.astype(BF16).reshape(depth, D_MODEL, n_ff, FF_CHUNK), (0, 2, 1, 3))
    w_down_bf = ffn_w_down.astype(BF16).reshape(depth, D_FF // FF_CHUNK, FF_CHUNK, D_MODEL)

    xs = x[0]
    for l in range(depth):
        lambda_init = 0.8 - 0.6 * math.exp(-0.3 * l)
        bs_full = jnp.repeat(jnp.transpose(gm_b_s[l]), GM_HEAD_DIM, axis=1)
        a, q, k, v, c = _mix_in(
            l, xs, row(pre_mix_g[l]), w_in_bf, wqvt_bf,
            row(gm_ln_g[l]), row(gm_ln_b[l]), gm_w_s[l], bs_full,
            _pad_rows(cv_dw_w[l], CV_HALO), row(cv_dw_b[l]), row(cv_ln_g[l]), row(cv_ln_b[l]), gmat)
        b = _diff_attn(rel_bias, q, k, v, row(da_lq1[l]), row(da_lk1[l]), row(da_lq2[l]), row(da_lk2[l]),
                       row(da_subln_g[l]), lambda_init)
        xs = _out_ffn(
            l, xs, a, b, c, w_out_bf,
            row(post_mix_g[l]), row(pre_ffn_g[l]), row(post_ffn_g[l]),
            w_up_bf,
            jnp.transpose(_pad_rows(ffn_conv_w[l], V7X_SUBLANES).reshape(V7X_SUBLANES, n_ff, FF_CHUNK), (1, 0, 2)),
            ffn_conv_b[l].reshape(n_ff, 1, FF_CHUNK),
            w_down_bf)
    return xs[None]
```
